```python
import math
import jax, jax.numpy as jnp
from jax import lax
import numpy as np

D_MODEL = 1024
BATCH = 1
SEQ = 16384
DEPTH = 1

D_HEAD = 64
H_SB = 8
H_DSA = 8
H_IDX = 8
D_IDX = 32
SB_W = H_SB * D_HEAD
DSA_W = H_DSA * D_HEAD
IDX_QW = H_IDX * D_IDX
IN_COLS = 3 * SB_W + 3 * DSA_W + IDX_QW + D_IDX + H_IDX + 2 * D_MODEL
D_FF = 4 * D_MODEL
N_META = 16
BLOCK_Q = 128
K_SEL_MAX = 256
ROPE_THETA = 500000.0
RMS_EPS = 1e-6

kernel_name = "hybrid_stickbreak_dsa_gated"


def rms_norm(x, g):
    xf = x.astype(jnp.float32)
    y = xf * lax.rsqrt(jnp.mean(xf * xf, axis=-1, keepdims=True) + RMS_EPS)
    return (y * g.astype(jnp.float32)).astype(x.dtype)


def partial_rope(x, pos):
    rot = x.shape[-1] // 4
    half = rot // 2
    inv_freq = jnp.power(jnp.float32(ROPE_THETA), -jnp.arange(half, dtype=jnp.float32) * (2.0 / rot))
    ang = pos.astype(jnp.float32)[:, None] * inv_freq[None, :]
    cos = jnp.cos(ang)[None, :, None, :]
    sin = jnp.sin(ang)[None, :, None, :]
    xr = x[..., :rot].astype(jnp.float32)
    x1, x2 = xr[..., :half], xr[..., half:]
    rotated = jnp.concatenate([x1 * cos - x2 * sin, x2 * cos + x1 * sin], axis=-1)
    return jnp.concatenate([rotated.astype(x.dtype), x[..., rot:]], axis=-1)


def split_offsets():
    sizes = (SB_W, SB_W, SB_W, DSA_W, DSA_W, DSA_W, IDX_QW, D_IDX, H_IDX, D_MODEL, D_MODEL)
    offs, acc = [], 0
    for s in sizes[:-1]:
        acc += s
        offs.append(acc)
    return offs


def stick_breaking_block(start, q, k, v):
    tp = k.shape[1]
    qb = lax.dynamic_slice_in_dim(q, start, BLOCK_Q, axis=1)
    t = start + jnp.arange(BLOCK_Q)
    s = jnp.arange(tp)
    z = jnp.einsum('bqhd,bkhd->bhqk', qb, k) * (D_HEAD ** -0.5)
    visible = (s[None, :] < t[:, None])[None, None]
    log_not = jnp.where(visible, -jax.nn.softplus(z), 0.0)
    suffix = lax.cumsum(log_not, axis=3, reverse=True)
    excl = jnp.concatenate([suffix[..., 1:], jnp.zeros_like(suffix[..., :1])], axis=-1)
    a = jnp.where(visible, jnp.exp(jax.nn.log_sigmoid(z) + excl), 0.0)
    return jnp.einsum('bhqk,bkhd->bqhd', a, v)


def dsa_block(start, q, k, v, q_idx, k_idx, w_idx, k_top):
    tp = k.shape[1]
    qb = lax.dynamic_slice_in_dim(q, start, BLOCK_Q, axis=1)
    qib = lax.dynamic_slice_in_dim(q_idx, start, BLOCK_Q, axis=1)
    wib = lax.dynamic_slice_in_dim(w_idx, start, BLOCK_Q, axis=1)
    t = start + jnp.arange(BLOCK_Q)
    s = jnp.arange(tp)
    rel = jax.nn.relu(jnp.einsum('bqhd,bkd->bqhk', qib, k_idx) * (D_IDX ** -0.5))
    score = jnp.einsum('bqhk,bqh->bqk', rel, wib)
    score = jnp.where((s[None, :] <= t[:, None])[None], score, -jnp.inf)
    _, idx = lax.top_k(score, k_top)
    sel_ok = (idx <= t[None, :, None])
    kg = jax.vmap(lambda kk, ii: kk[ii])(k, idx)
    vg = jax.vmap(lambda vv, ii: vv[ii])(v, idx)
    logits = jnp.einsum('bqhd,bqkhd->bhqk', qb, kg) * (D_HEAD ** -0.5)
    logits = jnp.where(sel_ok[:, None], logits, -jnp.inf)
    p = jax.nn.softmax(logits, axis=-1)
    return jnp.einsum('bhqk,bqkhd->bqhd', p, vg)


def hybrid_mixer(h, w_in, b_gate, w_branch_sb, w_branch_dsa, w_out, k_top):
    b, tp, _ = h.shape
    f32 = jnp.float32
    proj = h @ w_in
    (q_sb, k_sb, v_sb, q_ds, k_ds, v_ds, q_ix, k_ix, w_ix, g_sb, g_ds) = jnp.split(proj, split_offsets(), axis=-1)
    pos = jnp.arange(tp)
    q_sb = q_sb.reshape(b, tp, H_SB, D_HEAD).astype(f32)
    k_sb = k_sb.reshape(b, tp, H_SB, D_HEAD).astype(f32)
    v_sb = v_sb.reshape(b, tp, H_SB, D_HEAD).astype(f32)
    q_ds = partial_rope(q_ds.reshape(b, tp, H_DSA, D_HEAD), pos).astype(f32)
    k_ds = partial_rope(k_ds.reshape(b, tp, H_DSA, D_HEAD), pos).astype(f32)
    v_ds = v_ds.reshape(b, tp, H_DSA, D_HEAD).astype(f32)
    q_ix = partial_rope(q_ix.reshape(b, tp, H_IDX, D_IDX), pos).astype(f32)
    k_ix = partial_rope(k_ix.reshape(b, tp, 1, D_IDX), pos)[:, :, 0].astype(f32)
    w_ix = w_ix.astype(f32) * (H_IDX ** -0.5)

    starts = jnp.arange(tp // BLOCK_Q) * BLOCK_Q
    o_sb = lax.map(lambda st: stick_breaking_block(st, q_sb, k_sb, v_sb), starts)
    o_ds = lax.map(lambda st: dsa_block(st, q_ds, k_ds, v_ds, q_ix, k_ix, w_ix, k_top), starts)
    o_sb = jnp.transpose(o_sb, (1, 0, 2, 3, 4)).reshape(b, tp, SB_W).astype(h.dtype)
    o_ds = jnp.transpose(o_ds, (1, 0, 2, 3, 4)).reshape(b, tp, DSA_W).astype(h.dtype)

    gates = jax.nn.sigmoid(jnp.concatenate([g_sb, g_ds], axis=-1) + b_gate)
    gate_sb, gate_ds = gates[..., :D_MODEL], gates[..., D_MODEL:]
    merged = gate_sb * (o_sb @ w_branch_sb) + gate_ds * (o_ds @ w_branch_dsa)
    return merged @ w_out


def sq_relu_mlp(h, w1, w2):
    u = jax.nn.relu(h @ w1)
    return (u * u) @ w2


def setup_inputs(seed: int = 0) -> dict:
    key = jax.random.key(seed)
    ks = jax.random.split(key, 16)
    nrm = lambda k, shape, scale: jax.random.normal(k, shape, jnp.float32) * scale
    return {
        "x": nrm(ks[0], (BATCH, SEQ, D_MODEL), 1.0),
        "meta_tokens": nrm(ks[1], (N_META, D_MODEL), 1.0),
        "w_in": nrm(ks[2], (DEPTH, D_MODEL, IN_COLS), D_MODEL ** -0.5),
        "b_gate": nrm(ks[3], (DEPTH, 2 * D_MODEL), 0.1),
        "w_branch_sb": nrm(ks[4], (DEPTH, SB_W, D_MODEL), SB_W ** -0.5),
        "w_branch_dsa": nrm(ks[5], (DEPTH, DSA_W, D_MODEL), DSA_W ** -0.5),
        "w_out": nrm(ks[6], (DEPTH, D_MODEL, D_MODEL), D_MODEL ** -0.5),
        "g_mix_pre": 1.0 + nrm(ks[7], (DEPTH, D_MODEL), 0.02),
        "g_mix_post": 1.0 + nrm(ks[8], (DEPTH, D_MODEL), 0.02),
        "w_mlp_in": nrm(ks[9], (DEPTH, D_MODEL, D_FF), D_MODEL ** -0.5),
        "w_mlp_out": nrm(ks[10], (DEPTH, D_FF, D_MODEL), D_FF ** -0.5),
        "g_mlp_pre": 1.0 + nrm(ks[11], (DEPTH, D_MODEL), 0.02),
        "g_mlp_post": 1.0 + nrm(ks[12], (DEPTH, D_MODEL), 0.02),
    }


def reference(x, meta_tokens, w_in, b_gate, w_branch_sb, w_branch_dsa, w_out,
              g_mix_pre, g_mix_post, w_mlp_in, w_mlp_out, g_mlp_pre, g_mlp_post):
    b, seq, d = x.shape
    k_top = min(K_SEL_MAX, seq // 4)
    meta = jnp.broadcast_to(meta_tokens.astype(x.dtype)[None], (b, N_META, d))
    h = jnp.concatenate([meta, x], axis=1)
    t_real = seq + N_META
    tp = ((t_real + BLOCK_Q - 1) // BLOCK_Q) * BLOCK_Q
    h = jnp.pad(h, ((0, 0), (0, tp - t_real), (0, 0)))
    for l in range(DEPTH):
        mix = hybrid_mixer(rms_norm(h, g_mix_pre[l]), w_in[l], b_gate[l],
                           w_branch_sb[l], w_branch_dsa[l], w_out[l], k_top)
        h = h + rms_norm(mix, g_mix_post[l])
        ff = sq_relu_mlp(rms_norm(h, g_mlp_pre[l]), w_mlp_in[l], w_mlp_out[l])
        h = h + rms_norm(ff, g_mlp_post[l])
    return h[:, N_META:N_META + seq]
```

```python
import functools

import jax
import jax.numpy as jnp
from jax import lax
from jax.experimental import pallas as pl
from jax.experimental.pallas import tpu as pltpu

D_MODEL = 1024
D_HEAD = 64
N_HEADS = 8
N_PAIRS = N_HEADS // 2
D_IDX = 32
N_META = 16
K_SEL_MAX = 256
D_FF = 4 * D_MODEL
ROPE_THETA = 500000.0
RMS_EPS = 1e-6

LANES = 128
BLK = 256
SEQ_ALIGN = 512
VMEM_LIMIT = 56 * 1024 * 1024

EXP_ZERO_BELOW = -104.0
NEG_INF = float("-inf")
INT_MIN = -2 ** 31
KEY_OF_NEG_INF = 0x007FFFFF

F32 = jnp.float32
BF16 = jnp.bfloat16


def _dot(a, b):
    return jnp.dot(a, b, preferred_element_type=F32)


def _dot_nt(a, b):
    return lax.dot_general(a, b, (((1,), (1,)), ((), ())), preferred_element_type=F32)


def _rms(x, g):
    return x * lax.rsqrt(jnp.mean(x * x, axis=-1, keepdims=True) + RMS_EPS) * g


def _params(n_axes=1):
    return pltpu.CompilerParams(
        dimension_semantics=("arbitrary",) * n_axes, vmem_limit_bytes=VMEM_LIMIT)


def _resident():
    return pl.BlockSpec(memory_space=pltpu.VMEM)


def _prenorm_kernel(h_ref, g_ref, o_ref):
    o_ref[...] = _rms(h_ref[...], g_ref[...]).astype(BF16)


def _prenorm(h, g):
    tp = h.shape[0]
    tm = SEQ_ALIGN
    return pl.pallas_call(
        _prenorm_kernel,
        grid=(tp // tm,),
        in_specs=[pl.BlockSpec((tm, D_MODEL), lambda i: (i, 0)),
                  pl.BlockSpec((1, D_MODEL), lambda i: (0, 0))],
        out_specs=pl.BlockSpec((tm, D_MODEL), lambda i: (i, 0)),
        out_shape=jax.ShapeDtypeStruct((tp, D_MODEL), BF16),
        compiler_params=_params(),
        name="prenorm",
    )(h, g)


N_PLAIN = 3
N_ROPE = 7


def _proj_rows_kernel(x_ref, wp_ref, wr_ref, wrot_ref, cos_ref, sin_ref, ww_ref, wg_ref, bg_ref,
                      qsb_ref, vsb_ref, vds_ref, qds_ref, qix_ref, kix_ref, wix_ref, gate_ref):
    x = x_ref[...]
    y = _dot(x, wp_ref[...])
    for n, ref in enumerate((qsb_ref, vsb_ref, vds_ref)):
        for g in range(N_PAIRS):
            c = (n * N_PAIRS + g) * LANES
            ref[g] = y[:, c:c + LANES].astype(BF16)
    a = _dot(x, wr_ref[...])
    b = _dot(x, wrot_ref[...])
    cos, sin = cos_ref[...], sin_ref[...]
    cds, sds = cos[:, :LANES], sin[:, :LANES]
    cix, six = cos[:, LANES:], sin[:, LANES:]

    def rope(g, c, s):
        sl = slice(g * LANES, (g + 1) * LANES)
        return (a[:, sl] * c + b[:, sl] * s).astype(BF16)

    for g in range(N_PAIRS):
        qds_ref[g] = rope(g, cds, sds)
    for g in range(2):
        qix_ref[g] = rope(N_PAIRS + g, cix, six)
    kix_ref[...] = rope(N_PAIRS + 2, cix, six)
    wix_ref[...] = _dot(x, ww_ref[...])
    gate_ref[...] = jax.nn.sigmoid(_dot(x, wg_ref[...]) + bg_ref[...]).astype(BF16)


def _proj_rows(hn, wp, wr, wrot, cos, sin, ww, wg, bg):
    tp = hn.shape[0]
    tm = BLK
    full = lambda a: pl.BlockSpec(a.shape, lambda i: (0,) * a.ndim)
    grp = lambda n: pl.BlockSpec((n, tm, LANES), lambda i: (0, i, 0))
    return pl.pallas_call(
        _proj_rows_kernel,
        grid=(tp // tm,),
        in_specs=[pl.BlockSpec((tm, D_MODEL), lambda i: (i, 0)), full(wp), full(wr), full(wrot),
                  pl.BlockSpec((tm, 2 * LANES), lambda i: (i, 0)),
                  pl.BlockSpec((tm, 2 * LANES), lambda i: (i, 0)),
                  full(ww), full(wg), full(bg)],
        out_specs=[grp(N_PAIRS)] * N_PLAIN + [grp(N_PAIRS), grp(2),
                   pl.BlockSpec((tm, LANES), lambda i: (i, 0)),
                   pl.BlockSpec((tm, LANES), lambda i: (i, 0)),
                   pl.BlockSpec((tm, 2 * D_MODEL), lambda i: (i, 0))],
        out_shape=[jax.ShapeDtypeStruct((N_PAIRS, tp, LANES), BF16)] * N_PLAIN + [
                   jax.ShapeDtypeStruct((N_PAIRS, tp, LANES), BF16),
                   jax.ShapeDtypeStruct((2, tp, LANES), BF16),
                   jax.ShapeDtypeStruct((tp, LANES), BF16),
                   jax.ShapeDtypeStruct((tp, LANES), F32),
                   jax.ShapeDtypeStruct((tp, 2 * D_MODEL), BF16)],
        compiler_params=_params(),
        name="proj_rows",
    )(hn, wp, wr, wrot, cos, sin, ww, wg, bg)


def _proj_cols_kernel(x_ref, wp_ref, wr_ref, wrot_ref, cos_ref, sin_ref, ww_ref,
                      ksb_ref, kds_ref, qix_ref, kix_ref, wix_ref):
    x = x_ref[...]
    y = _dot_nt(wp_ref[...], x)
    for g in range(N_PAIRS):
        ksb_ref[g, 0] = y[g * LANES:(g + 1) * LANES, :].astype(BF16)
    a = _dot_nt(wr_ref[...], x)
    b = _dot_nt(wrot_ref[...], x)
    cos, sin = cos_ref[...], sin_ref[...]
    cds, sds = cos[:LANES, :], sin[:LANES, :]
    cix, six = cos[LANES:, :], sin[LANES:, :]

    def rope(g, c, s):
        sl = slice(g * LANES, (g + 1) * LANES)
        return (a[sl, :] * c + b[sl, :] * s).astype(BF16)

    for g in range(N_PAIRS):
        kds_ref[g, 0] = rope(g, cds, sds)
    for g in range(2):
        qix_ref[g] = rope(N_PAIRS + g, cix, six)
    kix_ref[0] = rope(N_PAIRS + 2, cix, six)
    wix_ref[...] = _dot_nt(ww_ref[...], x)


def _proj_cols(hn, wp, wr, wrot, cos, sin, ww):
    tp = hn.shape[0]
    nkb = tp // BLK
    full = lambda a: pl.BlockSpec(a.shape, lambda i: (0,) * a.ndim)
    kblk = pl.BlockSpec((N_PAIRS, 1, LANES, BLK), lambda i: (0, i, 0, 0))
    return pl.pallas_call(
        _proj_cols_kernel,
        grid=(nkb,),
        in_specs=[pl.BlockSpec((BLK, D_MODEL), lambda i: (i, 0)), full(wp), full(wr), full(wrot),
                  pl.BlockSpec((2 * LANES, BLK), lambda i: (0, i)),
                  pl.BlockSpec((2 * LANES, BLK), lambda i: (0, i)),
                  full(ww)],
        out_specs=[kblk, kblk,
                   pl.BlockSpec((2, LANES, BLK), lambda i: (0, 0, i)),
                   pl.BlockSpec((1, LANES, BLK), lambda i: (i, 0, 0)),
                   pl.BlockSpec((LANES, BLK), lambda i: (0, i))],
        out_shape=[jax.ShapeDtypeStruct((N_PAIRS, nkb, LANES, BLK), BF16),
                   jax.ShapeDtypeStruct((N_PAIRS, nkb, LANES, BLK), BF16),
                   jax.ShapeDtypeStruct((2, LANES, tp), BF16),
                   jax.ShapeDtypeStruct((nkb, LANES, BLK), BF16),
                   jax.ShapeDtypeStruct((LANES, tp), F32)],
        compiler_params=_params(),
        name="proj_cols",
    )(hn, wp, wr, wrot, cos, sin, ww)


def _split_pair_queries(q_ref, qm_scr):
    lane = lax.broadcasted_iota(jnp.int32, (BLK, LANES), 1)
    for j in range(N_PAIRS):
        qp = q_ref[j]
        zero = jnp.zeros_like(qp)
        qm_scr[2 * j] = jnp.where(lane < D_HEAD, qp, zero)
        qm_scr[2 * j + 1] = jnp.where(lane >= D_HEAD, qp, zero)


def _sb_kernel(q_ref, kt_ref, v_ref, u_ref, o_ref, qm_scr, acc_scr):
    i = pl.program_id(0)
    _split_pair_queries(q_ref, qm_scr)
    acc_scr[...] = jnp.zeros_like(acc_scr)
    row = lax.broadcasted_iota(jnp.int32, (BLK, BLK), 0)
    col = lax.broadcasted_iota(jnp.int32, (BLK, BLK), 1)
    visible_diag = col < row
    lane = lax.broadcasted_iota(jnp.int32, (BLK, LANES), 1)
    u = u_ref[...]

    for h in range(N_HEADS):
        j = h // 2
        mine = (lane < D_HEAD) if h % 2 == 0 else (lane >= D_HEAD)

        def block(kb, later, diag, h=h, j=j, mine=mine):
            z = _dot(qm_scr[h], kt_ref[j, kb])
            sp = jnp.maximum(z, 0.0) + jnp.log1p(jnp.exp(-jnp.abs(z)))
            log_not = -sp
            if diag:
                log_not = jnp.where(visible_diag, log_not, 0.0)
            hi = log_not.astype(BF16)
            lo = (log_not - hi.astype(F32)).astype(BF16)
            excl = _dot(hi, u) + _dot(lo, u)
            a = jnp.exp(z - sp + excl + later)
            if diag:
                a = jnp.where(visible_diag, a, 0.0)
            start = pl.multiple_of(kb * BLK, BLK)
            pv = _dot(a.astype(BF16), v_ref[j, pl.ds(start, BLK), :])
            acc_scr[j] += jnp.where(mine, pv, 0.0)
            return later + jnp.sum(log_not, axis=1, keepdims=True)

        later0 = block(i, jnp.zeros((BLK, 1), F32), True)

        def cond(carry):
            kb, _, worst = carry
            return jnp.logical_and(kb >= 0, worst > EXP_ZERO_BELOW)

        def body(carry):
            kb, later, _ = carry
            later = block(kb, later, False)
            return kb - 1, later, jnp.max(later)

        lax.while_loop(cond, body, (i - 1, later0, jnp.max(later0)))

    for j in range(N_PAIRS):
        o_ref[:, j * LANES:(j + 1) * LANES] = acc_scr[j].astype(BF16)


def _sb_attention(q, kt, v, u):
    tp = q.shape[1]
    return pl.pallas_call(
        _sb_kernel,
        grid=(tp // BLK,),
        in_specs=[pl.BlockSpec((N_PAIRS, BLK, LANES), lambda i: (0, i, 0)),
                  _resident(), _resident(), _resident()],
        out_specs=pl.BlockSpec((BLK, N_PAIRS * LANES), lambda i: (i, 0)),
        out_shape=jax.ShapeDtypeStruct((tp, N_PAIRS * LANES), BF16),
        scratch_shapes=[pltpu.VMEM((N_HEADS, BLK, LANES), BF16),
                        pltpu.VMEM((N_PAIRS, BLK, LANES), F32)],
        compiler_params=_params(),
        name="sb_attention",
    )(q, kt, v, u)


def _key_to_float(cand):
    neg_top = cand < 0
    bits = jnp.where(neg_top, cand ^ jnp.int32(INT_MIN), ~cand)
    f = lax.bitcast_convert_type(bits, F32)
    return jnp.where(jnp.logical_or(neg_top, cand > KEY_OF_NEG_INF), f, NEG_INF)


def _select_kernel(kix_ref, qixt_ref, wixt_ref, thr_ref, jst_ref, s_scr, rhs_scr, *, k_top, t_real):
    i = pl.program_id(0)
    nblk = i + 1
    sub = lax.broadcasted_iota(jnp.int32, (LANES, BLK), 0)
    for h in range(N_HEADS):
        src = qixt_ref[h // 4]
        rhs_scr[h] = jnp.where(sub // D_IDX == h % 4, src, jnp.zeros_like(src))
    w = wixt_ref[0:N_HEADS, :]
    srow = lax.broadcasted_iota(jnp.int32, (BLK, BLK), 0)
    tq = i * BLK + lax.broadcasted_iota(jnp.int32, (BLK, BLK), 1)

    def fill(kb, carry):
        start = pl.multiple_of(kb * BLK, BLK)
        kx = kix_ref[pl.ds(start, BLK), :]
        sc = None
        for h in range(N_HEADS):
            term = w[h:h + 1, :] * jnp.maximum(_dot(kx, rhs_scr[h]), 0.0)
            sc = term if sc is None else sc + term
        s_scr[pl.ds(start, BLK), :] = jnp.where(start + srow <= tq, sc, NEG_INF)
        return carry

    lax.fori_loop(0, nblk, fill, 0)

    def reduce_blocks(fn, init, combine):
        def body(kb, acc):
            start = pl.multiple_of(kb * BLK, BLK)
            blk = s_scr[pl.ds(start, BLK), :].reshape(BLK // 8, 8, BLK)
            return combine(acc, fn(blk, start))
        return lax.fori_loop(0, nblk, body, init)

    def fold4(m):
        return m.reshape(8, 4, 8, BLK).sum(axis=0)

    def total(acc):
        t = acc.sum(axis=0).sum(axis=0, keepdims=True)
        return jnp.broadcast_to(t, (8, BLK))

    zeros4 = jnp.zeros((4, 8, BLK), F32)
    add = lambda a, b: a + b

    def count(pred):
        return total(reduce_blocks(lambda blk, st: fold4(jnp.where(pred(blk, st), 1.0, 0.0)), zeros4, add))

    k_f = jnp.float32(k_top)

    def bisect(it, c):
        cand = c | lax.shift_left(jnp.int32(1), 31 - it)
        thr = _key_to_float(cand)
        cnt = count(lambda blk, st: blk >= thr[None])
        return jnp.where(cnt >= k_f, cand, c)

    c = lax.fori_loop(0, 32, bisect, jnp.zeros((8, BLK), jnp.int32))
    thr0 = _key_to_float(c)
    mins = reduce_blocks(
        lambda blk, st: jnp.where(blk >= thr0[None], blk, jnp.inf).reshape(8, 4, 8, BLK).min(axis=0),
        jnp.full((4, 8, BLK), jnp.inf, F32), jnp.minimum)
    thr = jnp.broadcast_to(mins.min(axis=0).min(axis=0, keepdims=True), (8, BLK))
    c_gt = count(lambda blk, st: blk > thr[None])
    c_eq = count(lambda blk, st: blk == thr[None])
    need = k_f - c_gt
    tlane = i * BLK + lax.broadcasted_iota(jnp.int32, (8, BLK), 1)
    ambiguous = (c_eq > need) & (thr > NEG_INF) & (tlane < t_real)
    thr_ref[...] = thr[0:1, :]
    no_limit = jnp.full((8, BLK), 2 ** 30, jnp.int32)
    jst_ref[...] = no_limit[0:1, :]

    @pl.when(jnp.max(jnp.where(ambiguous, 1.0, 0.0)) > 0.0)
    def _():
        sidx = lax.broadcasted_iota(jnp.int32, (BLK // 8, 8, BLK), 0) * 8 + \
            lax.broadcasted_iota(jnp.int32, (BLK // 8, 8, BLK), 1)

        def bisect_idx(it, jc):
            cand = jc | lax.shift_left(jnp.int32(1), 14 - it)
            cnt = count(lambda blk, st: (blk == thr[None]) & (st + sidx < cand[None]))
            return jnp.where(cnt < need, cand, jc)

        jc = lax.fori_loop(0, 15, bisect_idx, jnp.zeros((8, BLK), jnp.int32))
        jst_ref[...] = jnp.where(ambiguous, jc, no_limit)[0:1, :]


def _select(kix, qixt, wixt, k_top, t_real):
    tp = kix.shape[0]
    assert tp < 2 ** 15
    kern = functools.partial(_select_kernel, k_top=k_top, t_real=t_real)
    return pl.pallas_call(
        kern,
        grid=(tp // BLK,),
        in_specs=[_resident(),
                  pl.BlockSpec((2, LANES, BLK), lambda i: (0, 0, i)),
                  pl.BlockSpec((LANES, BLK), lambda i: (0, i))],
        out_specs=[pl.BlockSpec((1, BLK), lambda i: (0, i)),
                   pl.BlockSpec((1, BLK), lambda i: (0, i))],
        out_shape=[jax.ShapeDtypeStruct((1, tp), F32),
                   jax.ShapeDtypeStruct((1, tp), jnp.int32)],
        scratch_shapes=[pltpu.VMEM((tp, BLK), F32),
                        pltpu.VMEM((N_HEADS, LANES, BLK), BF16)],
        compiler_params=_params(),
        name="select_threshold",
    )(kix, qixt, wixt)


def _dsa_kernel(q_ref, kt_ref, v_ref, qix_ref, kixt_ref, wix_ref, thr_ref, jst_ref, o_ref,
                qm_scr, qixm_scr, wb_scr, thrb_scr, jstb_scr, m_scr, l_scr, acc_scr):
    i = pl.program_id(0)
    _split_pair_queries(q_ref, qm_scr)
    lane = lax.broadcasted_iota(jnp.int32, (BLK, LANES), 1)
    for h in range(N_HEADS):
        src = qix_ref[h // 4]
        qixm_scr[h] = jnp.where(lane // D_IDX == h % 4, src, jnp.zeros_like(src))
        wb_scr[h] = jnp.broadcast_to(wix_ref[:, h:h + 1], (BLK, BLK))
    thrb_scr[...] = jnp.broadcast_to(thr_ref[...], (BLK, BLK))
    jstb_scr[...] = jnp.broadcast_to(jst_ref[...], (BLK, BLK))
    m_scr[...] = jnp.full_like(m_scr, NEG_INF)
    l_scr[...] = jnp.zeros_like(l_scr)
    acc_scr[...] = jnp.zeros_like(acc_scr)
    row = lax.broadcasted_iota(jnp.int32, (BLK, BLK), 0)
    col = lax.broadcasted_iota(jnp.int32, (BLK, BLK), 1)
    first_half = lane < D_HEAD

    def block(kb, diag):
        kx = kixt_ref[kb]
        sc = None
        for h in range(N_HEADS):
            term = wb_scr[h] * jnp.maximum(_dot(qixm_scr[h], kx), 0.0)
            sc = term if sc is None else sc + term
        thr = thrb_scr[...]
        sel = (sc > thr) | ((sc == thr) & (kb * BLK + col <= jstb_scr[...]))
        if diag:
            sel = sel & (col <= row)
        start = pl.multiple_of(kb * BLK, BLK)
        for j in range(N_PAIRS):
            kt = kt_ref[j, kb]
            vv = v_ref[j, pl.ds(start, BLK), :]
            alphas, pvs = [], []
            for h in (2 * j, 2 * j + 1):
                s = jnp.where(sel, _dot(qm_scr[h], kt), NEG_INF)
                m_old = m_scr[h]
                m_new = jnp.maximum(m_old, jnp.max(s, axis=1, keepdims=True))
                m_safe = jnp.where(m_new == NEG_INF, 0.0, m_new)
                p = jnp.exp(s - m_safe)
                alpha = jnp.exp(m_old - m_safe)
                l_scr[h] = alpha * l_scr[h] + jnp.sum(p, axis=1, keepdims=True)
                m_scr[h] = m_new
                alphas.append(alpha)
                pvs.append(_dot(p.astype(BF16), vv))
            scale = jnp.where(first_half, alphas[0], alphas[1])
            acc_scr[j] = acc_scr[j] * scale + jnp.where(first_half, pvs[0], pvs[1])

    def body(kb, carry):
        block(kb, False)
        return carry

    lax.fori_loop(0, i, body, 0)
    block(i, True)
    for j in range(N_PAIRS):
        denom = jnp.where(first_half, l_scr[2 * j], l_scr[2 * j + 1])
        o_ref[:, j * LANES:(j + 1) * LANES] = (acc_scr[j] / denom).astype(BF16)


def _dsa_attention(q, kt, v, qix, kixt, wix, thr, jst):
    tp = q.shape[1]
    col1 = lambda: pl.BlockSpec((BLK, 1), lambda i: (i, 0))
    return pl.pallas_call(
        _dsa_kernel,
        grid=(tp // BLK,),
        in_specs=[pl.BlockSpec((N_PAIRS, BLK, LANES), lambda i: (0, i, 0)),
                  _resident(), _resident(),
                  pl.BlockSpec((2, BLK, LANES), lambda i: (0, i, 0)),
                  _resident(),
                  pl.BlockSpec((BLK, LANES), lambda i: (i, 0)),
                  col1(), col1()],
        out_specs=pl.BlockSpec((BLK, N_PAIRS * LANES), lambda i: (i, 0)),
        out_shape=jax.ShapeDtypeStruct((tp, N_PAIRS * LANES), BF16),
        scratch_shapes=[pltpu.VMEM((N_HEADS, BLK, LANES), BF16),
                        pltpu.VMEM((N_HEADS, BLK, LANES), BF16),
                        pltpu.VMEM((N_HEADS, BLK, BLK), F32),
                        pltpu.VMEM((BLK, BLK), F32),
                        pltpu.VMEM((BLK, BLK), jnp.int32),
                        pltpu.VMEM((N_HEADS, BLK, 1), F32),
                        pltpu.VMEM((N_HEADS, BLK, 1), F32),
                        pltpu.VMEM((N_PAIRS, BLK, LANES), F32)],
        compiler_params=_params(),
        name="dsa_attention",
    )(q, kt, v, qix, kixt, wix, thr, jst)


def _merge_kernel(osb_ref, ods_ref, gate_ref, h_ref, wsb_ref, wds_ref, wout_ref, g_ref, o_ref):
    gate = gate_ref[...].astype(F32)
    merged = (gate[:, :D_MODEL] * _dot(osb_ref[...], wsb_ref[...])
              + gate[:, D_MODEL:] * _dot(ods_ref[...], wds_ref[...]))
    mix = _dot(merged.astype(BF16), wout_ref[...])
    o_ref[...] = h_ref[...] + _rms(mix, g_ref[...])


def _merge(osb, ods, gates, h, wsb, wds, wout, g):
    tp = h.shape[0]
    tm = BLK
    full = lambda a: pl.BlockSpec(a.shape, lambda i: (0,) * a.ndim)
    rows = lambda n: pl.BlockSpec((tm, n), lambda i: (i, 0))
    return pl.pallas_call(
        _merge_kernel,
        grid=(tp // tm,),
        in_specs=[rows(N_PAIRS * LANES), rows(N_PAIRS * LANES), rows(2 * D_MODEL), rows(D_MODEL),
                  full(wsb), full(wds), full(wout), full(g)],
        out_specs=rows(D_MODEL),
        out_shape=jax.ShapeDtypeStruct((tp, D_MODEL), F32),
        compiler_params=_params(),
        name="merge_out",
    )(osb, ods, gates, h, wsb, wds, wout, g)


def _mlp_kernel(h_ref, gpre_ref, w1_ref, w2_ref, gpost_ref, o_ref):
    h = h_ref[...]
    hn = _rms(h, gpre_ref[...]).astype(BF16)
    ff = None
    for c in range(D_FF // D_MODEL):
        sl = slice(c * D_MODEL, (c + 1) * D_MODEL)
        u = jnp.maximum(_dot(hn, w1_ref[:, sl]), 0.0)
        part = _dot((u * u).astype(BF16), w2_ref[sl, :])
        ff = part if ff is None else ff + part
    o_ref[...] = h + _rms(ff, gpost_ref[...])


def _mlp(h, gpre, w1, w2, gpost):
    tp = h.shape[0]
    tm = BLK
    full = lambda a: pl.BlockSpec(a.shape, lambda i: (0,) * a.ndim)
    rows = pl.BlockSpec((tm, D_MODEL), lambda i: (i, 0))
    return pl.pallas_call(
        _mlp_kernel,
        grid=(tp // tm,),
        in_specs=[rows, full(gpre), full(w1), full(w2), full(gpost)],
        out_specs=rows,
        out_shape=jax.ShapeDtypeStruct((tp, D_MODEL), F32),
        compiler_params=_params(),
        name="mlp",
    )(h, gpre, w1, w2, gpost)


def _rotate_half_cols(w, d_head):
    rot = d_head // 4
    half = rot // 2
    n = w.shape[1] // d_head
    w3 = w.reshape(w.shape[0], n, d_head)
    out = jnp.zeros_like(w3)
    out = out.at[:, :, :half].set(-w3[:, :, half:rot])
    out = out.at[:, :, half:rot].set(w3[:, :, :half])
    return out.reshape(w.shape)


def _rope_tables(tp, d_head):
    rot = d_head // 4
    half = rot // 2
    inv_freq = jnp.power(jnp.float32(ROPE_THETA), -jnp.arange(half, dtype=F32) * (2.0 / rot))
    ang = jnp.arange(tp).astype(F32)[:, None] * inv_freq[None, :]
    cos, sin = jnp.cos(ang), jnp.sin(ang)
    ones = jnp.ones((tp, d_head - rot), F32)
    cos_h = jnp.concatenate([cos, cos, ones], axis=1)
    sin_h = jnp.concatenate([sin, sin, jnp.zeros_like(ones)], axis=1)
    reps = LANES // d_head
    return jnp.tile(cos_h, (1, reps)), jnp.tile(sin_h, (1, reps))


def kernel(x, meta_tokens, w_in, b_gate, w_branch_sb, w_branch_dsa, w_out, g_mix_pre, g_mix_post,
           w_mlp_in, w_mlp_out, g_mlp_pre, g_mlp_post):
    b, seq, d = x.shape
    assert b == 1 and d == D_MODEL and w_in.shape[0] == 1
    k_top = min(K_SEL_MAX, seq // 4)
    t_real = seq + N_META
    tp = -(-t_real // SEQ_ALIGN) * SEQ_ALIGN
    h = jnp.concatenate([meta_tokens.astype(x.dtype), x[0],
                         jnp.zeros((tp - t_real, d), x.dtype)], axis=0)

    w = w_in[0]
    hw = N_HEADS * D_HEAD
    iq = N_HEADS * D_IDX
    o = 0
    w_qsb, w_ksb, w_vsb = w[:, o:o + hw], w[:, o + hw:o + 2 * hw], w[:, o + 2 * hw:o + 3 * hw]
    o += 3 * hw
    w_qds, w_kds, w_vds = w[:, o:o + hw], w[:, o + hw:o + 2 * hw], w[:, o + 2 * hw:o + 3 * hw]
    o += 3 * hw
    w_qix, w_kix, w_wix = w[:, o:o + iq], w[:, o + iq:o + iq + D_IDX], w[:, o + iq + D_IDX:o + iq + D_IDX + N_HEADS]
    o += iq + D_IDX + N_HEADS
    w_gate = w[:, o:o + 2 * D_MODEL]
    w_qsb = w_qsb * (D_HEAD ** -0.5)
    w_qds = w_qds * (D_HEAD ** -0.5)
    w_wix = jnp.pad(w_wix * ((D_IDX * N_HEADS) ** -0.5), ((0, 0), (0, LANES - N_HEADS)))
    w_kix4 = jnp.tile(w_kix, (1, LANES // D_IDX))

    cos_ds, sin_ds = _rope_tables(tp, D_HEAD)
    cos_ix, sin_ix = _rope_tables(tp, D_IDX)
    cos = jnp.concatenate([cos_ds, cos_ix], axis=1)
    sin = jnp.concatenate([sin_ds, sin_ix], axis=1)

    bf = lambda a: a.astype(BF16)
    wp_r = bf(jnp.concatenate([w_qsb, w_vsb, w_vds], axis=1))
    wr_r = jnp.concatenate([w_qds, w_qix, w_kix4], axis=1)
    wrot_r = jnp.concatenate([_rotate_half_cols(w_qds, D_HEAD), _rotate_half_cols(w_qix, D_IDX),
                              _rotate_half_cols(w_kix4, D_IDX)], axis=1)
    wp_c = bf(w_ksb.T)
    wr_c = jnp.concatenate([w_kds, w_qix, w_kix4], axis=1)
    wrot_c = jnp.concatenate([_rotate_half_cols(w_kds, D_HEAD), _rotate_half_cols(w_qix, D_IDX),
                              _rotate_half_cols(w_kix4, D_IDX)], axis=1)

    hn = _prenorm(h, g_mix_pre)
    q_sb, v_sb, v_ds, q_ds, q_ix, k_ix4, w_ix, gates = _proj_rows(
        hn, wp_r, bf(wr_r), bf(wrot_r), cos, sin, bf(w_wix), bf(w_gate), b_gate)
    kt_sb, kt_ds, q_ixt, k_ixt4, w_ixt = _proj_cols(
        hn, wp_c, bf(wr_c.T), bf(wrot_c.T), cos.T, sin.T, bf(w_wix.T))

    later_keys = (jnp.arange(BLK)[:, None] > jnp.arange(BLK)[None, :]).astype(BF16)
    o_sb = _sb_attention(q_sb, kt_sb, v_sb, later_keys)

    thr, jst = _select(k_ix4, q_ixt, w_ixt, k_top, t_real)
    o_ds = _dsa_attention(q_ds, kt_ds, v_ds, q_ix, k_ixt4, w_ix,
                          thr.reshape(tp, 1), jst.reshape(tp, 1))

    h1 = _merge(o_sb, o_ds, gates, h, bf(w_branch_sb[0]), bf(w_branch_dsa[0]), bf(w_out[0]), g_mix_post)
    h2 = _mlp(h1, g_mlp_pre, bf(w_mlp_in[0]), bf(w_mlp_out[0]), g_mlp_post)
    return h2[None, N_META:N_META + seq]
```

```python
import functools

import jax
import jax.numpy as jnp
from jax import lax
from jax.experimental import pallas as pl
from jax.experimental.pallas import tpu as pltpu

D_MODEL = 1024
D_HEAD = 64
N_HEADS = 8
N_PAIRS = N_HEADS // 2
D_IDX = 32
N_META = 16
K_SEL_MAX = 256
D_FF = 4 * D_MODEL
ROPE_THETA = 500000.0
RMS_EPS = 1e-6

LANES = 128
BLK = 256
SEQ_ALIGN = 512
VMEM_LIMIT = 60 * 1024 * 1024

EXP_ZERO_BELOW = -104.0
NEG_INF = float("-inf")
INT_MIN = -2 ** 31
KEY_OF_NEG_INF = 0x007FFFFF
SUM_SAFE_LO = 1e-30
SUM_SAFE_HI = 1e30

F32 = jnp.float32
BF16 = jnp.bfloat16


def _dot(a, b):
    return jnp.dot(a, b, preferred_element_type=F32)


def _dot_nt(a, b):
    return lax.dot_general(a, b, (((1,), (1,)), ((), ())), preferred_element_type=F32)


def _rms(x, g):
    return x * lax.rsqrt(jnp.mean(x * x, axis=-1, keepdims=True) + RMS_EPS) * g


def _params(n_axes=1):
    return pltpu.CompilerParams(
        dimension_semantics=("arbitrary",) * n_axes, vmem_limit_bytes=VMEM_LIMIT)


def _resident():
    return pl.BlockSpec(memory_space=pltpu.VMEM)


def _prenorm_kernel(h_ref, g_ref, o_ref):
    o_ref[...] = _rms(h_ref[...], g_ref[...]).astype(BF16)


def _prenorm(h, g):
    tp = h.shape[0]
    tm = SEQ_ALIGN
    return pl.pallas_call(
        _prenorm_kernel,
        grid=(tp // tm,),
        in_specs=[pl.BlockSpec((tm, D_MODEL), lambda i: (i, 0)),
                  pl.BlockSpec((1, D_MODEL), lambda i: (0, 0))],
        out_specs=pl.BlockSpec((tm, D_MODEL), lambda i: (i, 0)),
        out_shape=jax.ShapeDtypeStruct((tp, D_MODEL), BF16),
        compiler_params=_params(),
        name="prenorm",
    )(h, g)


def _proj_rows_kernel(x_ref, wp_ref, wr_ref, wrot_ref, cos_ref, sin_ref, wg_ref, bg_ref,
                      qsb_ref, vsb_ref, vds_ref, qds_ref, kix_ref, gate_ref):
    x = x_ref[...]
    y = _dot(x, wp_ref[...])
    for n, ref in enumerate((qsb_ref, vsb_ref, vds_ref)):
        for g in range(N_PAIRS):
            c = (n * N_PAIRS + g) * LANES
            ref[g] = y[:, c:c + LANES].astype(BF16)
    a = _dot(x, wr_ref[...])
    b = _dot(x, wrot_ref[...])
    cos, sin = cos_ref[...], sin_ref[...]
    cds, sds = cos[:, :LANES], sin[:, :LANES]
    cix, six = cos[:, LANES:], sin[:, LANES:]

    def rope(g, c, s):
        sl = slice(g * LANES, (g + 1) * LANES)
        return (a[:, sl] * c + b[:, sl] * s).astype(BF16)

    for g in range(N_PAIRS):
        qds_ref[g] = rope(g, cds, sds)
    kix_ref[...] = rope(N_PAIRS, cix, six)
    gate_ref[...] = jax.nn.sigmoid(_dot(x, wg_ref[...]) + bg_ref[...]).astype(BF16)


def _proj_rows(hn, wp, wr, wrot, cos, sin, wg, bg):
    tp = hn.shape[0]
    tm = BLK
    full = lambda a: pl.BlockSpec(a.shape, lambda i: (0,) * a.ndim)
    grp = pl.BlockSpec((N_PAIRS, tm, LANES), lambda i: (0, i, 0))
    grp_shape = jax.ShapeDtypeStruct((N_PAIRS, tp, LANES), BF16)
    return pl.pallas_call(
        _proj_rows_kernel,
        grid=(tp // tm,),
        in_specs=[pl.BlockSpec((tm, D_MODEL), lambda i: (i, 0)), full(wp), full(wr), full(wrot),
                  pl.BlockSpec((tm, 2 * LANES), lambda i: (i, 0)),
                  pl.BlockSpec((tm, 2 * LANES), lambda i: (i, 0)),
                  full(wg), full(bg)],
        out_specs=[grp, grp, grp, grp,
                   pl.BlockSpec((tm, LANES), lambda i: (i, 0)),
                   pl.BlockSpec((tm, 2 * D_MODEL), lambda i: (i, 0))],
        out_shape=[grp_shape, grp_shape, grp_shape, grp_shape,
                   jax.ShapeDtypeStruct((tp, LANES), BF16),
                   jax.ShapeDtypeStruct((tp, 2 * D_MODEL), BF16)],
        compiler_params=_params(),
        name="proj_rows",
    )(hn, wp, wr, wrot, cos, sin, wg, bg)


def _proj_cols_kernel(x_ref, wp_ref, wr_ref, wrot_ref, cos_ref, sin_ref, ww_ref,
                      ksb_ref, kds_ref, qix_ref, wix_ref):
    x = x_ref[...]
    y = _dot_nt(wp_ref[...], x)
    for g in range(N_PAIRS):
        ksb_ref[g, 0] = y[g * LANES:(g + 1) * LANES, :].astype(BF16)
    a = _dot_nt(wr_ref[...], x)
    b = _dot_nt(wrot_ref[...], x)
    cos, sin = cos_ref[...], sin_ref[...]
    cds, sds = cos[:LANES, :], sin[:LANES, :]
    cix, six = cos[LANES:, :], sin[LANES:, :]

    def rope(g, c, s):
        sl = slice(g * LANES, (g + 1) * LANES)
        return (a[sl, :] * c + b[sl, :] * s).astype(BF16)

    for g in range(N_PAIRS):
        kds_ref[g, 0] = rope(g, cds, sds)
    for g in range(2):
        qix_ref[g] = rope(N_PAIRS + g, cix, six)
    wix_ref[...] = _dot_nt(ww_ref[...], x)


def _proj_cols(hn, wp, wr, wrot, cos, sin, ww):
    tp = hn.shape[0]
    nkb = tp // BLK
    full = lambda a: pl.BlockSpec(a.shape, lambda i: (0,) * a.ndim)
    kblk = pl.BlockSpec((N_PAIRS, 1, LANES, BLK), lambda i: (0, i, 0, 0))
    return pl.pallas_call(
        _proj_cols_kernel,
        grid=(nkb,),
        in_specs=[pl.BlockSpec((BLK, D_MODEL), lambda i: (i, 0)), full(wp), full(wr), full(wrot),
                  pl.BlockSpec((2 * LANES, BLK), lambda i: (0, i)),
                  pl.BlockSpec((2 * LANES, BLK), lambda i: (0, i)),
                  full(ww)],
        out_specs=[kblk, kblk,
                   pl.BlockSpec((2, LANES, BLK), lambda i: (0, 0, i)),
                   pl.BlockSpec((LANES, BLK), lambda i: (0, i))],
        out_shape=[jax.ShapeDtypeStruct((N_PAIRS, nkb, LANES, BLK), BF16),
                   jax.ShapeDtypeStruct((N_PAIRS, nkb, LANES, BLK), BF16),
                   jax.ShapeDtypeStruct((2, LANES, tp), BF16),
                   jax.ShapeDtypeStruct((LANES, tp), F32)],
        compiler_params=_params(),
        name="proj_cols",
    )(hn, wp, wr, wrot, cos, sin, ww)


def _split_pair_queries(q_ref, qm_scr):
    lane = lax.broadcasted_iota(jnp.int32, (BLK, LANES), 1)
    for j in range(N_PAIRS):
        qp = q_ref[j]
        zero = jnp.zeros_like(qp)
        qm_scr[2 * j] = jnp.where(lane < D_HEAD, qp, zero)
        qm_scr[2 * j + 1] = jnp.where(lane >= D_HEAD, qp, zero)


def _sb_kernel(q_ref, kt_ref, v_ref, u_ref, o_ref, qm_scr, acc_scr):
    i = pl.program_id(0)
    _split_pair_queries(q_ref, qm_scr)
    acc_scr[...] = jnp.zeros_like(acc_scr)
    row = lax.broadcasted_iota(jnp.int32, (BLK, BLK), 0)
    col = lax.broadcasted_iota(jnp.int32, (BLK, BLK), 1)
    visible_diag = col < row
    lane = lax.broadcasted_iota(jnp.int32, (BLK, LANES), 1)
    u = u_ref[...]

    for h in range(N_HEADS):
        j = h // 2
        mine = (lane < D_HEAD) if h % 2 == 0 else (lane >= D_HEAD)

        def block(kb, later, diag, h=h, j=j, mine=mine):
            z = _dot(qm_scr[h], kt_ref[j, kb])
            sp = jnp.maximum(z, 0.0) + jnp.log1p(jnp.exp(-jnp.abs(z)))
            log_not = -sp
            if diag:
                log_not = jnp.where(visible_diag, log_not, 0.0)
            hi = log_not.astype(BF16)
            lo = (log_not - hi.astype(F32)).astype(BF16)
            excl = _dot(hi, u) + _dot(lo, u)
            a = jnp.exp(z - sp + excl + later)
            if diag:
                a = jnp.where(visible_diag, a, 0.0)
            start = pl.multiple_of(kb * BLK, BLK)
            pv = _dot(a.astype(BF16), v_ref[j, pl.ds(start, BLK), :])
            acc_scr[j] += jnp.where(mine, pv, 0.0)
            return later + jnp.sum(log_not, axis=1, keepdims=True)

        later0 = block(i, jnp.zeros((BLK, 1), F32), True)

        def cond(carry):
            kb, _, worst = carry
            return jnp.logical_and(kb >= 0, worst > EXP_ZERO_BELOW)

        def body(carry):
            kb, later, _ = carry
            later = block(kb, later, False)
            return kb - 1, later, jnp.max(later)

        lax.while_loop(cond, body, (i - 1, later0, jnp.max(later0)))

    for j in range(N_PAIRS):
        o_ref[:, j * LANES:(j + 1) * LANES] = acc_scr[j].astype(BF16)


def _sb_attention(q, kt, v, u):
    tp = q.shape[1]
    return pl.pallas_call(
        _sb_kernel,
        grid=(tp // BLK,),
        in_specs=[pl.BlockSpec((N_PAIRS, BLK, LANES), lambda i: (0, i, 0)),
                  _resident(), _resident(), _resident()],
        out_specs=pl.BlockSpec((BLK, N_PAIRS * LANES), lambda i: (i, 0)),
        out_shape=jax.ShapeDtypeStruct((tp, N_PAIRS * LANES), BF16),
        scratch_shapes=[pltpu.VMEM((N_HEADS, BLK, LANES), BF16),
                        pltpu.VMEM((N_PAIRS, BLK, LANES), F32)],
        compiler_params=_params(),
        name="sb_attention",
    )(q, kt, v, u)


def _key_to_float(cand):
    neg_top = cand < 0
    bits = jnp.where(neg_top, cand ^ jnp.int32(INT_MIN), ~cand)
    f = lax.bitcast_convert_type(bits, F32)
    return jnp.where(jnp.logical_or(neg_top, cand > KEY_OF_NEG_INF), f, NEG_INF)


def _select_kernel(kix_ref, qixt_ref, wixt_ref, mask_ref, s_scr, rhs_scr, jst_scr, *, k_top, t_real):
    i = pl.program_id(0)
    nblk = i + 1
    nkb = mask_ref.shape[1]
    sub = lax.broadcasted_iota(jnp.int32, (LANES, BLK), 0)
    for h in range(N_HEADS):
        src = qixt_ref[h // 4]
        rhs_scr[h] = jnp.where(sub // D_IDX == h % 4, src, jnp.zeros_like(src))
    w = wixt_ref[0:N_HEADS, :]
    srow = lax.broadcasted_iota(jnp.int32, (BLK, BLK), 0)
    tq = i * BLK + lax.broadcasted_iota(jnp.int32, (BLK, BLK), 1)

    def fill(kb, carry):
        start = pl.multiple_of(kb * BLK, BLK)
        kx = kix_ref[pl.ds(start, BLK), :]
        sc = None
        for h in range(N_HEADS):
            term = w[h:h + 1, :] * jnp.maximum(_dot(kx, rhs_scr[h]), 0.0)
            sc = term if sc is None else sc + term
        s_scr[pl.ds(start, BLK), :] = jnp.where(start + srow <= tq, sc, NEG_INF)
        return carry

    lax.fori_loop(0, nblk, fill, 0)

    def reduce_blocks(fn, init, combine):
        def body(kb, acc):
            start = pl.multiple_of(kb * BLK, BLK)
            blk = s_scr[pl.ds(start, BLK), :].reshape(BLK // 8, 8, BLK)
            return combine(acc, fn(blk, start))
        return lax.fori_loop(0, nblk, body, init)

    def fold4(m):
        return m.reshape(8, 4, 8, BLK).sum(axis=0)

    def total(acc):
        t = acc.sum(axis=0).sum(axis=0, keepdims=True)
        return jnp.broadcast_to(t, (8, BLK))

    zeros4 = jnp.zeros((4, 8, BLK), F32)
    add = lambda a, b: a + b

    def count(pred):
        return total(reduce_blocks(lambda blk, st: fold4(jnp.where(pred(blk, st), 1.0, 0.0)), zeros4, add))

    k_f = jnp.float32(k_top)

    def bisect(it, c):
        cand = c | lax.shift_left(jnp.int32(1), 31 - it)
        thr = _key_to_float(cand)
        cnt = count(lambda blk, st: blk >= thr[None])
        return jnp.where(cnt >= k_f, cand, c)

    c = lax.fori_loop(0, 32, bisect, jnp.zeros((8, BLK), jnp.int32))
    thr0 = _key_to_float(c)
    mins = reduce_blocks(
        lambda blk, st: jnp.where(blk >= thr0[None], blk, jnp.inf).reshape(8, 4, 8, BLK).min(axis=0),
        jnp.full((4, 8, BLK), jnp.inf, F32), jnp.minimum)
    thr = jnp.broadcast_to(mins.min(axis=0).min(axis=0, keepdims=True), (8, BLK))
    both = reduce_blocks(
        lambda blk, st: jnp.stack([fold4(jnp.where(blk > thr[None], 1.0, 0.0)),
                                   fold4(jnp.where(blk == thr[None], 1.0, 0.0))]),
        jnp.zeros((2, 4, 8, BLK), F32), add)
    c_gt, c_eq = total(both[0]), total(both[1])
    need = k_f - c_gt
    tlane = i * BLK + lax.broadcasted_iota(jnp.int32, (8, BLK), 1)
    finite_thr = thr > NEG_INF
    ambiguous = (c_eq > need) & finite_thr & (tlane < t_real)
    jst_scr[...] = jnp.where(finite_thr, jnp.int32(2 ** 30), jnp.int32(-1))

    @pl.when(jnp.max(jnp.where(ambiguous, 1.0, 0.0)) > 0.0)
    def _():
        sidx = lax.broadcasted_iota(jnp.int32, (BLK // 8, 8, BLK), 0) * 8 + \
            lax.broadcasted_iota(jnp.int32, (BLK // 8, 8, BLK), 1)

        def bisect_idx(it, jc):
            cand = jc | lax.shift_left(jnp.int32(1), 14 - it)
            cnt = count(lambda blk, st: (blk == thr[None]) & (st + sidx < cand[None]))
            return jnp.where(cnt < need, cand, jc)

        jc = lax.fori_loop(0, 15, bisect_idx, jnp.zeros((8, BLK), jnp.int32))
        jst_scr[...] = jnp.where(ambiguous, jc, jst_scr[...])

    thr_row = thr[0:1, :]
    jst_row = jst_scr[0:1, :]

    def emit(kb, carry):
        start = pl.multiple_of(kb * BLK, BLK)
        blk = s_scr[pl.ds(start, BLK), :]
        sel = (blk > thr_row) | ((blk == thr_row) & (start + srow <= jst_row))
        mask_ref[0, kb] = jnp.where(sel, 0.0, NEG_INF).T.astype(BF16)
        return carry

    lax.fori_loop(0, nblk, emit, 0)

    def blank(kb, carry):
        mask_ref[0, kb] = jnp.full((BLK, BLK), NEG_INF, BF16)
        return carry

    lax.fori_loop(nblk, nkb, blank, 0)


def _select(kix, qixt, wixt, k_top, t_real):
    tp = kix.shape[0]
    assert tp < 2 ** 15
    nkb = tp // BLK
    kern = functools.partial(_select_kernel, k_top=k_top, t_real=t_real)
    return pl.pallas_call(
        kern,
        grid=(nkb,),
        in_specs=[_resident(),
                  pl.BlockSpec((2, LANES, BLK), lambda i: (0, 0, i)),
                  pl.BlockSpec((LANES, BLK), lambda i: (0, i))],
        out_specs=pl.BlockSpec((1, nkb, BLK, BLK), lambda i: (i, 0, 0, 0)),
        out_shape=jax.ShapeDtypeStruct((nkb, nkb, BLK, BLK), BF16),
        scratch_shapes=[pltpu.VMEM((tp, BLK), F32),
                        pltpu.VMEM((N_HEADS, LANES, BLK), BF16),
                        pltpu.VMEM((8, BLK), jnp.int32)],
        compiler_params=_params(),
        name="select_mask",
    )(kix, qixt, wixt)


def _dsa_kernel(q_ref, kt_ref, v_ref, mask_ref, o_ref, qm_scr, shift_scr, lsum_scr, acc_scr):
    i = pl.program_id(0)
    _split_pair_queries(q_ref, qm_scr)
    lane = lax.broadcasted_iota(jnp.int32, (BLK, LANES), 1)
    first_half = lane < D_HEAD

    def logits(h, kt, bias):
        return _dot(qm_scr[h], kt) + bias

    def accumulate(shifted):
        lsum_scr[...] = jnp.zeros_like(lsum_scr)
        acc_scr[...] = jnp.zeros_like(acc_scr)

        def body(kb, carry):
            bias = mask_ref[0, kb].astype(F32)
            start = pl.multiple_of(kb * BLK, BLK)
            for j in range(N_PAIRS):
                kt = kt_ref[j, kb]
                vv = v_ref[j, pl.ds(start, BLK), :]
                pvs = []
                for h in (2 * j, 2 * j + 1):
                    s = logits(h, kt, bias)
                    if shifted:
                        m = shift_scr[h]
                        s = s - jnp.concatenate([m, m], axis=1)
                    p = jnp.exp(s)
                    lsum_scr[h] += p[:, :LANES] + p[:, LANES:]
                    pvs.append(_dot(p.astype(BF16), vv))
                acc_scr[j] += jnp.where(first_half, pvs[0], pvs[1])
            return carry

        lax.fori_loop(0, i + 1, body, 0)

    def row_sums():
        return [jnp.sum(lsum_scr[h], axis=1, keepdims=True) for h in range(N_HEADS)]

    accumulate(False)
    unsafe = jnp.zeros((BLK, 1), F32)
    for l in row_sums():
        unsafe = jnp.maximum(unsafe, jnp.where((l > SUM_SAFE_LO) & (l < SUM_SAFE_HI), 0.0, 1.0))
    acc_big = jnp.where(jnp.abs(acc_scr[...]) < SUM_SAFE_HI, 0.0, 1.0)

    @pl.when((jnp.max(unsafe) > 0.0) | (jnp.max(acc_big) > 0.0))
    def _():
        shift_scr[...] = jnp.full_like(shift_scr, NEG_INF)

        def body(kb, carry):
            bias = mask_ref[0, kb].astype(F32)
            for j in range(N_PAIRS):
                kt = kt_ref[j, kb]
                for h in (2 * j, 2 * j + 1):
                    s = logits(h, kt, bias)
                    shift_scr[h] = jnp.maximum(shift_scr[h], jnp.maximum(s[:, :LANES], s[:, LANES:]))
            return carry

        lax.fori_loop(0, i + 1, body, 0)
        for h in range(N_HEADS):
            m = jnp.max(shift_scr[h], axis=1, keepdims=True)
            m = jnp.where(m == NEG_INF, 0.0, m)
            shift_scr[h] = jnp.broadcast_to(m, (BLK, LANES))
        accumulate(True)

    sums = row_sums()
    for j in range(N_PAIRS):
        denom = jnp.where(first_half, sums[2 * j], sums[2 * j + 1])
        o_ref[:, j * LANES:(j + 1) * LANES] = (acc_scr[j] / denom).astype(BF16)


def _dsa_attention(q, kt, v, mask):
    tp = q.shape[1]
    nkb = tp // BLK
    return pl.pallas_call(
        _dsa_kernel,
        grid=(nkb,),
        in_specs=[pl.BlockSpec((N_PAIRS, BLK, LANES), lambda i: (0, i, 0)),
                  _resident(), _resident(),
                  pl.BlockSpec((1, nkb, BLK, BLK), lambda i: (i, 0, 0, 0))],
        out_specs=pl.BlockSpec((BLK, N_PAIRS * LANES), lambda i: (i, 0)),
        out_shape=jax.ShapeDtypeStruct((tp, N_PAIRS * LANES), BF16),
        scratch_shapes=[pltpu.VMEM((N_HEADS, BLK, LANES), BF16),
                        pltpu.VMEM((N_HEADS, BLK, LANES), F32),
                        pltpu.VMEM((N_HEADS, BLK, LANES), F32),
                        pltpu.VMEM((N_PAIRS, BLK, LANES), F32)],
        compiler_params=_params(),
        name="dsa_attention",
    )(q, kt, v, mask)


def _merge_kernel(osb_ref, ods_ref, gate_ref, h_ref, wsb_ref, wds_ref, wout_ref, g_ref, o_ref):
    gate = gate_ref[...].astype(F32)
    merged = (gate[:, :D_MODEL] * _dot(osb_ref[...], wsb_ref[...])
              + gate[:, D_MODEL:] * _dot(ods_ref[...], wds_ref[...]))
    mix = _dot(merged.astype(BF16), wout_ref[...])
    o_ref[...] = h_ref[...] + _rms(mix, g_ref[...])


def _merge(osb, ods, gates, h, wsb, wds, wout, g):
    tp = h.shape[0]
    tm = BLK
    full = lambda a: pl.BlockSpec(a.shape, lambda i: (0,) * a.ndim)
    rows = lambda n: pl.BlockSpec((tm, n), lambda i: (i, 0))
    return pl.pallas_call(
        _merge_kernel,
        grid=(tp // tm,),
        in_specs=[rows(N_PAIRS * LANES), rows(N_PAIRS * LANES), rows(2 * D_MODEL), rows(D_MODEL),
                  full(wsb), full(wds), full(wout), full(g)],
        out_specs=rows(D_MODEL),
        out_shape=jax.ShapeDtypeStruct((tp, D_MODEL), F32),
        compiler_params=_params(),
        name="merge_out",
    )(osb, ods, gates, h, wsb, wds, wout, g)


def _mlp_kernel(h_ref, gpre_ref, w1_ref, w2_ref, gpost_ref, o_ref):
    h = h_ref[...]
    hn = _rms(h, gpre_ref[...]).astype(BF16)
    ff = None
    for c in range(D_FF // D_MODEL):
        sl = slice(c * D_MODEL, (c + 1) * D_MODEL)
        u = jnp.maximum(_dot(hn, w1_ref[:, sl]), 0.0)
        part = _dot((u * u).astype(BF16), w2_ref[sl, :])
        ff = part if ff is None else ff + part
    o_ref[...] = h + _rms(ff, gpost_ref[...])


def _mlp(h, gpre, w1, w2, gpost):
    tp = h.shape[0]
    tm = BLK
    full = lambda a: pl.BlockSpec(a.shape, lambda i: (0,) * a.ndim)
    rows = pl.BlockSpec((tm, D_MODEL), lambda i: (i, 0))
    return pl.pallas_call(
        _mlp_kernel,
        grid=(tp // tm,),
        in_specs=[rows, full(gpre), full(w1), full(w2), full(gpost)],
        out_specs=rows,
        out_shape=jax.ShapeDtypeStruct((tp, D_MODEL), F32),
        compiler_params=_params(),
        name="mlp",
    )(h, gpre, w1, w2, gpost)


def _rotate_half_cols(w, d_head):
    rot = d_head // 4
    half = rot // 2
    n = w.shape[1] // d_head
    w3 = w.reshape(w.shape[0], n, d_head)
    out = jnp.zeros_like(w3)
    out = out.at[:, :, :half].set(-w3[:, :, half:rot])
    out = out.at[:, :, half:rot].set(w3[:, :, :half])
    return out.reshape(w.shape)


def _rope_tables(tp, d_head):
    rot = d_head // 4
    half = rot // 2
    inv_freq = jnp.power(jnp.float32(ROPE_THETA), -jnp.arange(half, dtype=F32) * (2.0 / rot))
    ang = jnp.arange(tp).astype(F32)[:, None] * inv_freq[None, :]
    cos, sin = jnp.cos(ang), jnp.sin(ang)
    ones = jnp.ones((tp, d_head - rot), F32)
    cos_h = jnp.concatenate([cos, cos, ones], axis=1)
    sin_h = jnp.concatenate([sin, sin, jnp.zeros_like(ones)], axis=1)
    reps = LANES // d_head
    return jnp.tile(cos_h, (1, reps)), jnp.tile(sin_h, (1, reps))


def kernel(x, meta_tokens, w_in, b_gate, w_branch_sb, w_branch_dsa, w_out, g_mix_pre, g_mix_post,
           w_mlp_in, w_mlp_out, g_mlp_pre, g_mlp_post):
    b, seq, d = x.shape
    assert b == 1 and d == D_MODEL and w_in.shape[0] == 1
    k_top = min(K_SEL_MAX, seq // 4)
    t_real = seq + N_META
    tp = -(-t_real // SEQ_ALIGN) * SEQ_ALIGN
    h = jnp.concatenate([meta_tokens.astype(x.dtype), x[0],
                         jnp.zeros((tp - t_real, d), x.dtype)], axis=0)

    w = w_in[0]
    hw = N_HEADS * D_HEAD
    iq = N_HEADS * D_IDX
    o = 0
    w_qsb, w_ksb, w_vsb = w[:, o:o + hw], w[:, o + hw:o + 2 * hw], w[:, o + 2 * hw:o + 3 * hw]
    o += 3 * hw
    w_qds, w_kds, w_vds = w[:, o:o + hw], w[:, o + hw:o + 2 * hw], w[:, o + 2 * hw:o + 3 * hw]
    o += 3 * hw
    w_qix, w_kix, w_wix = w[:, o:o + iq], w[:, o + iq:o + iq + D_IDX], w[:, o + iq + D_IDX:o + iq + D_IDX + N_HEADS]
    o += iq + D_IDX + N_HEADS
    w_gate = w[:, o:o + 2 * D_MODEL]
    w_qsb = w_qsb * (D_HEAD ** -0.5)
    w_qds = w_qds * (D_HEAD ** -0.5)
    w_wix = jnp.pad(w_wix * ((D_IDX * N_HEADS) ** -0.5), ((0, 0), (0, LANES - N_HEADS)))
    w_kix4 = jnp.tile(w_kix, (1, LANES // D_IDX))

    cos_ds, sin_ds = _rope_tables(tp, D_HEAD)
    cos_ix, sin_ix = _rope_tables(tp, D_IDX)
    cos = jnp.concatenate([cos_ds, cos_ix], axis=1)
    sin = jnp.concatenate([sin_ds, sin_ix], axis=1)

    bf = lambda a: a.astype(BF16)
    wp_r = bf(jnp.concatenate([w_qsb, w_vsb, w_vds], axis=1))
    wr_r = jnp.concatenate([w_qds, w_kix4], axis=1)
    wrot_r = jnp.concatenate([_rotate_half_cols(w_qds, D_HEAD), _rotate_half_cols(w_kix4, D_IDX)], axis=1)
    wp_c = bf(w_ksb.T)
    wr_c = jnp.concatenate([w_kds, w_qix], axis=1)
    wrot_c = jnp.concatenate([_rotate_half_cols(w_kds, D_HEAD), _rotate_half_cols(w_qix, D_IDX)], axis=1)

    hn = _prenorm(h, g_mix_pre)
    q_sb, v_sb, v_ds, q_ds, k_ix4, gates = _proj_rows(
        hn, wp_r, bf(wr_r), bf(wrot_r), cos, sin, bf(w_gate), b_gate)
    kt_sb, kt_ds, q_ixt, w_ixt = _proj_cols(
        hn, wp_c, bf(wr_c.T), bf(wrot_c.T), cos.T, sin.T, bf(w_wix.T))

    later_keys = (jnp.arange(BLK)[:, None] > jnp.arange(BLK)[None, :]).astype(BF16)
    o_sb = _sb_attention(q_sb, kt_sb, v_sb, later_keys)

    mask = _select(k_ix4, q_ixt, w_ixt, k_top, t_real)
    o_ds = _dsa_attention(q_ds, kt_ds, v_ds, mask)

    h1 = _merge(o_sb, o_ds, gates, h, bf(w_branch_sb[0]), bf(w_branch_dsa[0]), bf(w_out[0]), g_mix_post)
    h2 = _mlp(h1, g_mlp_pre, bf(w_mlp_in[0]), bf(w_mlp_out[0]), g_mlp_post)
    return h2[None, N_META:N_META + seq]
```

```python
import functools

import jax
import jax.numpy as jnp
from jax import lax
from jax.experimental import pallas as pl
from jax.experimental.pallas import tpu as pltpu

D_MODEL = 1024
D_HEAD = 64
N_HEADS = 8
N_PAIRS = N_HEADS // 2
D_IDX = 32
N_META = 16
K_SEL_MAX = 256
D_FF = 4 * D_MODEL
ROPE_THETA = 500000.0
RMS_EPS = 1e-6

LANES = 128
BLK = 256
SEQ_ALIGN = 512
VMEM_LIMIT = 60 * 1024 * 1024

EXP2_ZERO_BELOW = -151.0
LOG2_E = 1.4426950408889634
NEG_INF = float("-inf")
INT_MIN = -2 ** 31
KEY_OF_NEG_INF = 0x007FFFFF
SUM_SAFE_LO = 1e-30
SUM_SAFE_HI = 1e30

F32 = jnp.float32
BF16 = jnp.bfloat16


def _dot(a, b):
    return jnp.dot(a, b, preferred_element_type=F32)


def _dot_nt(a, b):
    return lax.dot_general(a, b, (((1,), (1,)), ((), ())), preferred_element_type=F32)


def _rms(x, g):
    return x * lax.rsqrt(jnp.mean(x * x, axis=-1, keepdims=True) + RMS_EPS) * g


def _params(n_axes=1):
    return pltpu.CompilerParams(
        dimension_semantics=("arbitrary",) * n_axes, vmem_limit_bytes=VMEM_LIMIT)


def _resident():
    return pl.BlockSpec(memory_space=pltpu.VMEM)


def _prenorm_kernel(h_ref, g_ref, o_ref):
    o_ref[...] = _rms(h_ref[...], g_ref[...]).astype(BF16)


def _prenorm(h, g):
    tp = h.shape[0]
    tm = SEQ_ALIGN
    return pl.pallas_call(
        _prenorm_kernel,
        grid=(tp // tm,),
        in_specs=[pl.BlockSpec((tm, D_MODEL), lambda i: (i, 0)),
                  pl.BlockSpec((1, D_MODEL), lambda i: (0, 0))],
        out_specs=pl.BlockSpec((tm, D_MODEL), lambda i: (i, 0)),
        out_shape=jax.ShapeDtypeStruct((tp, D_MODEL), BF16),
        compiler_params=_params(),
        name="prenorm",
    )(h, g)


def _proj_rows_kernel(x_ref, wp_ref, wr_ref, wrot_ref, cos_ref, sin_ref, wg_ref, bg_ref,
                      qsb_ref, vsb_ref, vds_ref, qds_ref, kix_ref, gate_ref):
    x = x_ref[...]
    y = _dot(x, wp_ref[...])
    for n, ref in enumerate((qsb_ref, vsb_ref, vds_ref)):
        for g in range(N_PAIRS):
            c = (n * N_PAIRS + g) * LANES
            ref[g] = y[:, c:c + LANES].astype(BF16)
    a = _dot(x, wr_ref[...])
    b = _dot(x, wrot_ref[...])
    cos, sin = cos_ref[...], sin_ref[...]
    cds, sds = cos[:, :LANES], sin[:, :LANES]
    cix, six = cos[:, LANES:], sin[:, LANES:]

    def rope(g, c, s):
        sl = slice(g * LANES, (g + 1) * LANES)
        return (a[:, sl] * c + b[:, sl] * s).astype(BF16)

    for g in range(N_PAIRS):
        qds_ref[g] = rope(g, cds, sds)
    kix_ref[...] = rope(N_PAIRS, cix, six)
    gate_ref[...] = jax.nn.sigmoid(_dot(x, wg_ref[...]) + bg_ref[...]).astype(BF16)


def _proj_rows(hn, wp, wr, wrot, cos, sin, wg, bg):
    tp = hn.shape[0]
    tm = BLK
    full = lambda a: pl.BlockSpec(a.shape, lambda i: (0,) * a.ndim)
    grp = pl.BlockSpec((N_PAIRS, tm, LANES), lambda i: (0, i, 0))
    grp_shape = jax.ShapeDtypeStruct((N_PAIRS, tp, LANES), BF16)
    return pl.pallas_call(
        _proj_rows_kernel,
        grid=(tp // tm,),
        in_specs=[pl.BlockSpec((tm, D_MODEL), lambda i: (i, 0)), full(wp), full(wr), full(wrot),
                  pl.BlockSpec((tm, 2 * LANES), lambda i: (i, 0)),
                  pl.BlockSpec((tm, 2 * LANES), lambda i: (i, 0)),
                  full(wg), full(bg)],
        out_specs=[grp, grp, grp, grp,
                   pl.BlockSpec((tm, LANES), lambda i: (i, 0)),
                   pl.BlockSpec((tm, 2 * D_MODEL), lambda i: (i, 0))],
        out_shape=[grp_shape, grp_shape, grp_shape, grp_shape,
                   jax.ShapeDtypeStruct((tp, LANES), BF16),
                   jax.ShapeDtypeStruct((tp, 2 * D_MODEL), BF16)],
        compiler_params=_params(),
        name="proj_rows",
    )(hn, wp, wr, wrot, cos, sin, wg, bg)


def _proj_cols_kernel(x_ref, wp_ref, wr_ref, wrot_ref, cos_ref, sin_ref, ww_ref,
                      ksb_ref, kds_ref, qix_ref, wix_ref):
    x = x_ref[...]
    y = _dot_nt(wp_ref[...], x)
    for g in range(N_PAIRS):
        ksb_ref[g, 0] = y[g * LANES:(g + 1) * LANES, :].astype(BF16)
    a = _dot_nt(wr_ref[...], x)
    b = _dot_nt(wrot_ref[...], x)
    cos, sin = cos_ref[...], sin_ref[...]
    cds, sds = cos[:LANES, :], sin[:LANES, :]
    cix, six = cos[LANES:, :], sin[LANES:, :]

    def rope(g, c, s):
        sl = slice(g * LANES, (g + 1) * LANES)
        return (a[sl, :] * c + b[sl, :] * s).astype(BF16)

    for g in range(N_PAIRS):
        kds_ref[g, 0] = rope(g, cds, sds)
    for g in range(2):
        qix_ref[g] = rope(N_PAIRS + g, cix, six)
    wix_ref[...] = _dot_nt(ww_ref[...], x)


def _proj_cols(hn, wp, wr, wrot, cos, sin, ww):
    tp = hn.shape[0]
    nkb = tp // BLK
    full = lambda a: pl.BlockSpec(a.shape, lambda i: (0,) * a.ndim)
    kblk = pl.BlockSpec((N_PAIRS, 1, LANES, BLK), lambda i: (0, i, 0, 0))
    return pl.pallas_call(
        _proj_cols_kernel,
        grid=(nkb,),
        in_specs=[pl.BlockSpec((BLK, D_MODEL), lambda i: (i, 0)), full(wp), full(wr), full(wrot),
                  pl.BlockSpec((2 * LANES, BLK), lambda i: (0, i)),
                  pl.BlockSpec((2 * LANES, BLK), lambda i: (0, i)),
                  full(ww)],
        out_specs=[kblk, kblk,
                   pl.BlockSpec((2, LANES, BLK), lambda i: (0, 0, i)),
                   pl.BlockSpec((LANES, BLK), lambda i: (0, i))],
        out_shape=[jax.ShapeDtypeStruct((N_PAIRS, nkb, LANES, BLK), BF16),
                   jax.ShapeDtypeStruct((N_PAIRS, nkb, LANES, BLK), BF16),
                   jax.ShapeDtypeStruct((2, LANES, tp), BF16),
                   jax.ShapeDtypeStruct((LANES, tp), F32)],
        compiler_params=_params(),
        name="proj_cols",
    )(hn, wp, wr, wrot, cos, sin, ww)


def _split_pair_queries(q_ref, qm_scr):
    lane = lax.broadcasted_iota(jnp.int32, (BLK, LANES), 1)
    for j in range(N_PAIRS):
        qp = q_ref[j]
        zero = jnp.zeros_like(qp)
        qm_scr[2 * j] = jnp.where(lane < D_HEAD, qp, zero)
        qm_scr[2 * j + 1] = jnp.where(lane >= D_HEAD, qp, zero)


def _sb_kernel(q_ref, kt_ref, v_ref, u_ref, o_ref, qm_scr, later_scr, acc_scr):
    i = pl.program_id(0)
    _split_pair_queries(q_ref, qm_scr)
    acc_scr[...] = jnp.zeros_like(acc_scr)
    later_scr[...] = jnp.zeros_like(later_scr)
    row = lax.broadcasted_iota(jnp.int32, (BLK, BLK), 0)
    col = lax.broadcasted_iota(jnp.int32, (BLK, BLK), 1)
    visible_diag = col < row
    lane = lax.broadcasted_iota(jnp.int32, (BLK, LANES), 1)
    first_half = lane < D_HEAD
    u = u_ref[...]

    def block(kb, diag):
        start = pl.multiple_of(kb * BLK, BLK)
        worst = None
        for j in range(N_PAIRS):
            kt = kt_ref[j, kb]
            vv = v_ref[j, pl.ds(start, BLK), :]
            pvs = []
            for h in (2 * j, 2 * j + 1):
                z = _dot(qm_scr[h], kt)
                sp = jnp.maximum(z, 0.0) + jnp.log2(1.0 + jnp.exp2(-jnp.abs(z)))
                log_not = -sp
                if diag:
                    log_not = jnp.where(visible_diag, log_not, 0.0)
                inside = _dot(log_not.astype(BF16), u)
                later = later_scr[h]
                a = jnp.exp2(z - sp + inside + jnp.concatenate([later, later], axis=1))
                if diag:
                    a = jnp.where(visible_diag, a, 0.0)
                pvs.append(_dot(a.astype(BF16), vv))
                later = later + jnp.broadcast_to(inside[:, 0:1] + log_not[:, 0:1], (BLK, LANES))
                later_scr[h] = later
                worst = later if worst is None else jnp.maximum(worst, later)
            acc_scr[j] += jnp.where(first_half, pvs[0], pvs[1])
        return jnp.max(worst)

    def cond(carry):
        kb, worst = carry
        return jnp.logical_and(kb >= 0, worst > EXP2_ZERO_BELOW)

    def body(carry):
        kb, _ = carry
        return kb - 1, block(kb, False)

    lax.while_loop(cond, body, (i - 1, block(i, True)))

    for j in range(N_PAIRS):
        o_ref[:, j * LANES:(j + 1) * LANES] = acc_scr[j].astype(BF16)


def _sb_attention(q, kt, v, u):
    tp = q.shape[1]
    return pl.pallas_call(
        _sb_kernel,
        grid=(tp // BLK,),
        in_specs=[pl.BlockSpec((N_PAIRS, BLK, LANES), lambda i: (0, i, 0)),
                  _resident(), _resident(), _resident()],
        out_specs=pl.BlockSpec((BLK, N_PAIRS * LANES), lambda i: (i, 0)),
        out_shape=jax.ShapeDtypeStruct((tp, N_PAIRS * LANES), BF16),
        scratch_shapes=[pltpu.VMEM((N_HEADS, BLK, LANES), BF16),
                        pltpu.VMEM((N_HEADS, BLK, LANES), F32),
                        pltpu.VMEM((N_PAIRS, BLK, LANES), F32)],
        compiler_params=_params(),
        name="sb_attention",
    )(q, kt, v, u)


def _key_to_float(cand):
    neg_top = cand < 0
    bits = jnp.where(neg_top, cand ^ jnp.int32(INT_MIN), ~cand)
    f = lax.bitcast_convert_type(bits, F32)
    return jnp.where(jnp.logical_or(neg_top, cand > KEY_OF_NEG_INF), f, NEG_INF)


def _key16_to_float(p):
    top = p >= 0x8000
    pattern = jnp.where(top, p ^ 0x8000, p ^ 0xFFFF)
    f = lax.bitcast_convert_type(jnp.left_shift(pattern, 16), F32)
    return jnp.where(jnp.logical_or(top, p > (KEY_OF_NEG_INF >> 16)), f, NEG_INF)


def _select_kernel(kix_ref, qixt_ref, wixt_ref, mask_ref, s_scr, shi_scr, rhs_scr, jst_scr, thr_scr,
                   cgt_scr, ceq_scr, *, k_top, t_real):
    i = pl.program_id(0)
    nblk = i + 1
    nkb = mask_ref.shape[1]
    sub = lax.broadcasted_iota(jnp.int32, (LANES, BLK), 0)
    for h in range(N_HEADS):
        src = qixt_ref[h // 4]
        rhs_scr[h] = jnp.where(sub // D_IDX == h % 4, src, jnp.zeros_like(src))
    w = wixt_ref[0:N_HEADS, :]
    srow = lax.broadcasted_iota(jnp.int32, (BLK, BLK), 0)
    tq = i * BLK + lax.broadcasted_iota(jnp.int32, (BLK, BLK), 1)

    def fill(kb, carry):
        start = pl.multiple_of(kb * BLK, BLK)
        kx = kix_ref[pl.ds(start, BLK), :]
        sc = None
        for h in range(N_HEADS):
            term = w[h:h + 1, :] * jnp.maximum(_dot(kx, rhs_scr[h]), 0.0)
            sc = term if sc is None else sc + term
        sc = jnp.where(start + srow <= tq, sc, NEG_INF)
        s_scr[pl.ds(start, BLK), :] = sc
        shi_scr[pl.ds(start, BLK), :] = sc.astype(BF16)
        return carry

    lax.fori_loop(0, nblk, fill, 0)

    def reduce_blocks(fn, init, combine):
        def body(kb, acc):
            start = pl.multiple_of(kb * BLK, BLK)
            blk = s_scr[pl.ds(start, BLK), :].reshape(BLK // 8, 8, BLK)
            return combine(acc, fn(blk, start))
        return lax.fori_loop(0, nblk, body, init)

    def fold4(m):
        return m.reshape(8, 4, 8, BLK).sum(axis=0)

    def total(acc):
        t = acc.sum(axis=0).sum(axis=0, keepdims=True)
        return jnp.broadcast_to(t, (8, BLK))

    zeros4 = jnp.zeros((4, 8, BLK), F32)
    add = lambda a, b: a + b

    def count(pred):
        return total(reduce_blocks(lambda blk, st: fold4(jnp.where(pred(blk, st), 1.0, 0.0)), zeros4, add))

    k_f = jnp.float32(k_top)

    one_b, zero_b = jnp.ones((), BF16), jnp.zeros((), BF16)

    def count_upper(thr_b):
        def body(kb, acc):
            start = pl.multiple_of(kb * BLK, BLK)
            blk = shi_scr[pl.ds(start, BLK), :].reshape(BLK // 16, 16, BLK)
            m = jnp.where(blk >= thr_b[None], one_b, zero_b)
            part = m[0]
            for r in range(1, BLK // 16):
                part = part + m[r]
            return acc + part.astype(F32)
        acc = lax.fori_loop(0, nblk, body, jnp.zeros((16, BLK), F32))
        return jnp.broadcast_to(acc.sum(axis=0, keepdims=True), (8, BLK))

    def bisect_upper(it, p):
        cand = p | lax.shift_left(jnp.int32(1), 15 - it)
        thr_b = jnp.broadcast_to(_key16_to_float(cand)[0:1, :], (16, BLK)).astype(BF16)
        return jnp.where(count_upper(thr_b) >= k_f, cand, p)

    p1 = lax.fori_loop(0, 16, bisect_upper, jnp.zeros((8, BLK), jnp.int32))
    edges = [_key16_to_float(p1 + d) for d in (-1, 0, 1)]
    counts3 = reduce_blocks(
        lambda blk, st: jnp.stack([fold4(jnp.where(blk >= e[None], 1.0, 0.0)) for e in edges]),
        jnp.zeros((3, 4, 8, BLK), F32), add)
    c_lo, c_mid, c_hi = total(counts3[0]), total(counts3[1]), total(counts3[2])
    lower = c_mid < k_f
    p = jnp.where(lower, p1 - 1, p1)
    cnt_at_p = jnp.where(lower, c_lo, c_mid)
    cnt_above = jnp.where(lower, c_mid, c_hi)
    bracketed = (c_lo >= k_f) & (c_hi < k_f)
    tlane = i * BLK + lax.broadcasted_iota(jnp.int32, (8, BLK), 1)
    open_row = tlane + 1 < k_top
    bucket_lo = _key16_to_float(p)
    bucket_hi = _key16_to_float(p + 1)
    key_lo = jnp.where(p >= 0x8000, jnp.left_shift(p, 16), jnp.left_shift(p, 16) | 0xFFFF)
    bucket_size = cnt_at_p - cnt_above
    need_in = k_f - cnt_above

    def collect(kb, st):
        m1, m2, m3 = st
        start = pl.multiple_of(kb * BLK, BLK)
        blk = s_scr[pl.ds(start, BLK), :].reshape(8, 4, 8, BLK)
        x = jnp.where((blk >= bucket_lo[None, None]) & (blk < bucket_hi[None, None]), blk, NEG_INF)
        for a in range(8):
            t = jnp.minimum(m1, x[a])
            m1 = jnp.maximum(m1, x[a])
            t2 = jnp.minimum(m2, t)
            m2 = jnp.maximum(m2, t)
            m3 = jnp.maximum(m3, t2)
        return m1, m2, m3

    empty = jnp.full((4, 8, BLK), NEG_INF, F32)
    cells = jnp.concatenate(lax.fori_loop(0, nblk, collect, (empty, empty, empty)), axis=0)

    def count_cells(pred):
        t = jnp.where(pred, 1.0, 0.0).sum(axis=0).sum(axis=0, keepdims=True)
        return jnp.broadcast_to(t, (8, BLK))

    kept = count_cells(cells > NEG_INF)
    redo = jnp.logical_not(open_row) & ((kept < bucket_size) | jnp.logical_not(bracketed))

    def bisect_cells(it, low):
        cand = low | lax.shift_left(jnp.int32(1), 15 - it)
        cnt = count_cells(cells >= _key_to_float(key_lo + cand)[None])
        return jnp.where(cnt >= need_in, cand, low)

    def key_float(key):
        return _key_to_float(jnp.where(open_row, jnp.int32(KEY_OF_NEG_INF), key))

    thr0 = key_float(key_lo + lax.fori_loop(0, 16, bisect_cells, jnp.zeros((8, BLK), jnp.int32)))
    thr_c = jnp.where(cells >= thr0[None], cells, jnp.inf).min(axis=0).min(axis=0, keepdims=True)
    thr_c = jnp.broadcast_to(thr_c, (8, BLK))
    thr_scr[...] = thr_c
    cgt_scr[...] = cnt_above + count_cells(cells > thr_c[None])
    ceq_scr[...] = count_cells(cells == thr_c[None])

    @pl.when(jnp.max(jnp.where(redo, 1.0, 0.0)) > 0.0)
    def _():
        def bisect_all(it, c):
            cand = c | lax.shift_left(jnp.int32(1), 31 - it)
            t = _key_to_float(cand)
            cnt = count(lambda blk, st: blk >= t[None])
            return jnp.where(cnt >= k_f, cand, c)

        t0 = key_float(lax.fori_loop(0, 32, bisect_all, jnp.zeros((8, BLK), jnp.int32)))
        mins = reduce_blocks(
            lambda blk, st: jnp.where(blk >= t0[None], blk, jnp.inf).reshape(8, 4, 8, BLK).min(axis=0),
            jnp.full((4, 8, BLK), jnp.inf, F32), jnp.minimum)
        t = jnp.broadcast_to(mins.min(axis=0).min(axis=0, keepdims=True), (8, BLK))
        both = reduce_blocks(
            lambda blk, st: jnp.stack([fold4(jnp.where(blk > t[None], 1.0, 0.0)),
                                       fold4(jnp.where(blk == t[None], 1.0, 0.0))]),
            jnp.zeros((2, 4, 8, BLK), F32), add)
        thr_scr[...] = t
        cgt_scr[...] = total(both[0])
        ceq_scr[...] = total(both[1])

    thr, c_gt, c_eq = thr_scr[...], cgt_scr[...], ceq_scr[...]
    need = k_f - c_gt
    finite_thr = thr > NEG_INF
    ambiguous = (c_eq > need) & finite_thr & (tlane < t_real)
    jst_scr[...] = jnp.where(finite_thr, jnp.int32(2 ** 30), jnp.int32(-1))

    @pl.when(jnp.max(jnp.where(ambiguous, 1.0, 0.0)) > 0.0)
    def _():
        sidx = lax.broadcasted_iota(jnp.int32, (BLK // 8, 8, BLK), 0) * 8 + \
            lax.broadcasted_iota(jnp.int32, (BLK // 8, 8, BLK), 1)

        def bisect_idx(it, jc):
            cand = jc | lax.shift_left(jnp.int32(1), 14 - it)
            cnt = count(lambda blk, st: (blk == thr[None]) & (st + sidx < cand[None]))
            return jnp.where(cnt < need, cand, jc)

        jc = lax.fori_loop(0, 15, bisect_idx, jnp.zeros((8, BLK), jnp.int32))
        jst_scr[...] = jnp.where(ambiguous, jc, jst_scr[...])

    thr_row = thr[0:1, :]
    jst_row = jst_scr[0:1, :]

    def emit(kb, carry):
        start = pl.multiple_of(kb * BLK, BLK)
        blk = s_scr[pl.ds(start, BLK), :]
        sel = (blk > thr_row) | ((blk == thr_row) & (start + srow <= jst_row))
        mask_ref[0, kb] = jnp.where(sel, 0.0, NEG_INF).T.astype(BF16)
        return carry

    lax.fori_loop(0, nblk, emit, 0)

    def blank(kb, carry):
        mask_ref[0, kb] = jnp.full((BLK, BLK), NEG_INF, BF16)
        return carry

    lax.fori_loop(nblk, nkb, blank, 0)


def _select(kix, qixt, wixt, k_top, t_real):
    tp = kix.shape[0]
    assert tp < 2 ** 15
    nkb = tp // BLK
    kern = functools.partial(_select_kernel, k_top=k_top, t_real=t_real)
    return pl.pallas_call(
        kern,
        grid=(nkb,),
        in_specs=[_resident(),
                  pl.BlockSpec((2, LANES, BLK), lambda i: (0, 0, i)),
                  pl.BlockSpec((LANES, BLK), lambda i: (0, i))],
        out_specs=pl.BlockSpec((1, nkb, BLK, BLK), lambda i: (i, 0, 0, 0)),
        out_shape=jax.ShapeDtypeStruct((nkb, nkb, BLK, BLK), BF16),
        scratch_shapes=[pltpu.VMEM((tp, BLK), F32),
                        pltpu.VMEM((tp, BLK), BF16),
                        pltpu.VMEM((N_HEADS, LANES, BLK), BF16),
                        pltpu.VMEM((8, BLK), jnp.int32),
                        pltpu.VMEM((8, BLK), F32),
                        pltpu.VMEM((8, BLK), F32),
                        pltpu.VMEM((8, BLK), F32)],
        compiler_params=_params(),
        name="select_mask",
    )(kix, qixt, wixt)


def _dsa_kernel(q_ref, kt_ref, v_ref, mask_ref, o_ref, qm_scr, shift_scr, lsum_scr, acc_scr):
    i = pl.program_id(0)
    _split_pair_queries(q_ref, qm_scr)
    lane = lax.broadcasted_iota(jnp.int32, (BLK, LANES), 1)
    first_half = lane < D_HEAD

    def logits(h, kt, bias):
        return _dot(qm_scr[h], kt) + bias

    def accumulate(shifted):
        lsum_scr[...] = jnp.zeros_like(lsum_scr)
        acc_scr[...] = jnp.zeros_like(acc_scr)

        def body(kb, carry):
            bias = mask_ref[0, kb].astype(F32)
            start = pl.multiple_of(kb * BLK, BLK)
            for j in range(N_PAIRS):
                kt = kt_ref[j, kb]
                vv = v_ref[j, pl.ds(start, BLK), :]
                pvs = []
                for h in (2 * j, 2 * j + 1):
                    s = logits(h, kt, bias)
                    if shifted:
                        m = shift_scr[h]
                        s = s - jnp.concatenate([m, m], axis=1)
                    p = jnp.exp(s)
                    lsum_scr[h] += p[:, :LANES] + p[:, LANES:]
                    pvs.append(_dot(p.astype(BF16), vv))
                acc_scr[j] += jnp.where(first_half, pvs[0], pvs[1])
            return carry

        lax.fori_loop(0, i + 1, body, 0)

    def row_sums():
        return [jnp.sum(lsum_scr[h], axis=1, keepdims=True) for h in range(N_HEADS)]

    accumulate(False)
    unsafe = jnp.zeros((BLK, 1), F32)
    for l in row_sums():
        unsafe = jnp.maximum(unsafe, jnp.where((l > SUM_SAFE_LO) & (l < SUM_SAFE_HI), 0.0, 1.0))
    acc_big = jnp.where(jnp.abs(acc_scr[...]) < SUM_SAFE_HI, 0.0, 1.0)

    @pl.when((jnp.max(unsafe) > 0.0) | (jnp.max(acc_big) > 0.0))
    def _():
        shift_scr[...] = jnp.full_like(shift_scr, NEG_INF)

        def body(kb, carry):
            bias = mask_ref[0, kb].astype(F32)
            for j in range(N_PAIRS):
                kt = kt_ref[j, kb]
                for h in (2 * j, 2 * j + 1):
                    s = logits(h, kt, bias)
                    shift_scr[h] = jnp.maximum(shift_scr[h], jnp.maximum(s[:, :LANES], s[:, LANES:]))
            return carry

        lax.fori_loop(0, i + 1, body, 0)
        for h in range(N_HEADS):
            m = jnp.max(shift_scr[h], axis=1, keepdims=True)
            m = jnp.where(m == NEG_INF, 0.0, m)
            shift_scr[h] = jnp.broadcast_to(m, (BLK, LANES))
        accumulate(True)

    sums = row_sums()
    for j in range(N_PAIRS):
        denom = jnp.where(first_half, sums[2 * j], sums[2 * j + 1])
        o_ref[:, j * LANES:(j + 1) * LANES] = (acc_scr[j] / denom).astype(BF16)


def _dsa_attention(q, kt, v, mask):
    tp = q.shape[1]
    nkb = tp // BLK
    return pl.pallas_call(
        _dsa_kernel,
        grid=(nkb,),
        in_specs=[pl.BlockSpec((N_PAIRS, BLK, LANES), lambda i: (0, i, 0)),
                  _resident(), _resident(),
                  pl.BlockSpec((1, nkb, BLK, BLK), lambda i: (i, 0, 0, 0))],
        out_specs=pl.BlockSpec((BLK, N_PAIRS * LANES), lambda i: (i, 0)),
        out_shape=jax.ShapeDtypeStruct((tp, N_PAIRS * LANES), BF16),
        scratch_shapes=[pltpu.VMEM((N_HEADS, BLK, LANES), BF16),
                        pltpu.VMEM((N_HEADS, BLK, LANES), F32),
                        pltpu.VMEM((N_HEADS, BLK, LANES), F32),
                        pltpu.VMEM((N_PAIRS, BLK, LANES), F32)],
        compiler_params=_params(),
        name="dsa_attention",
    )(q, kt, v, mask)


def _merge_kernel(osb_ref, ods_ref, gate_ref, h_ref, wsb_ref, wds_ref, wout_ref, g_ref, o_ref):
    gate = gate_ref[...].astype(F32)
    merged = (gate[:, :D_MODEL] * _dot(osb_ref[...], wsb_ref[...])
              + gate[:, D_MODEL:] * _dot(ods_ref[...], wds_ref[...]))
    mix = _dot(merged.astype(BF16), wout_ref[...])
    o_ref[...] = h_ref[...] + _rms(mix, g_ref[...])


def _merge(osb, ods, gates, h, wsb, wds, wout, g):
    tp = h.shape[0]
    tm = BLK
    full = lambda a: pl.BlockSpec(a.shape, lambda i: (0,) * a.ndim)
    rows = lambda n: pl.BlockSpec((tm, n), lambda i: (i, 0))
    return pl.pallas_call(
        _merge_kernel,
        grid=(tp // tm,),
        in_specs=[rows(N_PAIRS * LANES), rows(N_PAIRS * LANES), rows(2 * D_MODEL), rows(D_MODEL),
                  full(wsb), full(wds), full(wout), full(g)],
        out_specs=rows(D_MODEL),
        out_shape=jax.ShapeDtypeStruct((tp, D_MODEL), F32),
        compiler_params=_params(),
        name="merge_out",
    )(osb, ods, gates, h, wsb, wds, wout, g)


def _mlp_kernel(h_ref, gpre_ref, w1_ref, w2_ref, gpost_ref, o_ref):
    h = h_ref[...]
    hn = _rms(h, gpre_ref[...]).astype(BF16)
    ff = None
    for c in range(D_FF // D_MODEL):
        sl = slice(c * D_MODEL, (c + 1) * D_MODEL)
        u = jnp.maximum(_dot(hn, w1_ref[:, sl]), 0.0)
        part = _dot((u * u).astype(BF16), w2_ref[sl, :])
        ff = part if ff is None else ff + part
    o_ref[...] = h + _rms(ff, gpost_ref[...])


def _mlp(h, gpre, w1, w2, gpost):
    tp = h.shape[0]
    tm = BLK
    full = lambda a: pl.BlockSpec(a.shape, lambda i: (0,) * a.ndim)
    rows = pl.BlockSpec((tm, D_MODEL), lambda i: (i, 0))
    return pl.pallas_call(
        _mlp_kernel,
        grid=(tp // tm,),
        in_specs=[rows, full(gpre), full(w1), full(w2), full(gpost)],
        out_specs=rows,
        out_shape=jax.ShapeDtypeStruct((tp, D_MODEL), F32),
        compiler_params=_params(),
        name="mlp",
    )(h, gpre, w1, w2, gpost)


def _rotate_half_cols(w, d_head):
    rot = d_head // 4
    half = rot // 2
    n = w.shape[1] // d_head
    w3 = w.reshape(w.shape[0], n, d_head)
    out = jnp.zeros_like(w3)
    out = out.at[:, :, :half].set(-w3[:, :, half:rot])
    out = out.at[:, :, half:rot].set(w3[:, :, :half])
    return out.reshape(w.shape)


def _rope_tables(tp, d_head):
    rot = d_head // 4
    half = rot // 2
    inv_freq = jnp.power(jnp.float32(ROPE_THETA), -jnp.arange(half, dtype=F32) * (2.0 / rot))
    ang = jnp.arange(tp).astype(F32)[:, None] * inv_freq[None, :]
    cos, sin = jnp.cos(ang), jnp.sin(ang)
    ones = jnp.ones((tp, d_head - rot), F32)
    cos_h = jnp.concatenate([cos, cos, ones], axis=1)
    sin_h = jnp.concatenate([sin, sin, jnp.zeros_like(ones)], axis=1)
    reps = LANES // d_head
    return jnp.tile(cos_h, (1, reps)), jnp.tile(sin_h, (1, reps))


def kernel(x, meta_tokens, w_in, b_gate, w_branch_sb, w_branch_dsa, w_out, g_mix_pre, g_mix_post,
           w_mlp_in, w_mlp_out, g_mlp_pre, g_mlp_post):
    b, seq, d = x.shape
    assert b == 1 and d == D_MODEL and w_in.shape[0] == 1
    k_top = min(K_SEL_MAX, seq // 4)
    t_real = seq + N_META
    tp = -(-t_real // SEQ_ALIGN) * SEQ_ALIGN
    h = jnp.concatenate([meta_tokens.astype(x.dtype), x[0],
                         jnp.zeros((tp - t_real, d), x.dtype)], axis=0)

    w = w_in[0]
    hw = N_HEADS * D_HEAD
    iq = N_HEADS * D_IDX
    o = 0
    w_qsb, w_ksb, w_vsb = w[:, o:o + hw], w[:, o + hw:o + 2 * hw], w[:, o + 2 * hw:o + 3 * hw]
    o += 3 * hw
    w_qds, w_kds, w_vds = w[:, o:o + hw], w[:, o + hw:o + 2 * hw], w[:, o + 2 * hw:o + 3 * hw]
    o += 3 * hw
    w_qix, w_kix, w_wix = w[:, o:o + iq], w[:, o + iq:o + iq + D_IDX], w[:, o + iq + D_IDX:o + iq + D_IDX + N_HEADS]
    o += iq + D_IDX + N_HEADS
    w_gate = w[:, o:o + 2 * D_MODEL]
    w_qsb = w_qsb * (D_HEAD ** -0.5 * LOG2_E)
    w_qds = w_qds * (D_HEAD ** -0.5)
    w_wix = jnp.pad(w_wix * ((D_IDX * N_HEADS) ** -0.5), ((0, 0), (0, LANES - N_HEADS)))
    w_kix4 = jnp.tile(w_kix, (1, LANES // D_IDX))

    cos_ds, sin_ds = _rope_tables(tp, D_HEAD)
    cos_ix, sin_ix = _rope_tables(tp, D_IDX)
    cos = jnp.concatenate([cos_ds, cos_ix], axis=1)
    sin = jnp.concatenate([sin_ds, sin_ix], axis=1)

    bf = lambda a: a.astype(BF16)
    wp_r = bf(jnp.concatenate([w_qsb, w_vsb, w_vds], axis=1))
    wr_r = jnp.concatenate([w_qds, w_kix4], axis=1)
    wrot_r = jnp.concatenate([_rotate_half_cols(w_qds, D_HEAD), _rotate_half_cols(w_kix4, D_IDX)], axis=1)
    wp_c = bf(w_ksb.T)
    wr_c = jnp.concatenate([w_kds, w_qix], axis=1)
    wrot_c = jnp.concatenate([_rotate_half_cols(w_kds, D_HEAD), _rotate_half_cols(w_qix, D_IDX)], axis=1)

    hn = _prenorm(h, g_mix_pre)
    q_sb, v_sb, v_ds, q_ds, k_ix4, gates = _proj_rows(
        hn, wp_r, bf(wr_r), bf(wrot_r), cos, sin, bf(w_gate), b_gate)
    kt_sb, kt_ds, q_ixt, w_ixt = _proj_cols(
        hn, wp_c, bf(wr_c.T), bf(wrot_c.T), cos.T, sin.T, bf(w_wix.T))

    later_keys = (jnp.arange(BLK)[:, None] > jnp.arange(BLK)[None, :]).astype(BF16)
    o_sb = _sb_attention(q_sb, kt_sb, v_sb, later_keys)

    mask = _select(k_ix4, q_ixt, w_ixt, k_top, t_real)
    o_ds = _dsa_attention(q_ds, kt_ds, v_ds, mask)

    h1 = _merge(o_sb, o_ds, gates, h, bf(w_branch_sb[0]), bf(w_branch_dsa[0]), bf(w_out[0]), g_mix_post)
    h2 = _mlp(h1, g_mlp_pre, bf(w_mlp_in[0]), bf(w_mlp_out[0]), g_mlp_post)
    return h2[None, N_META:N_META + seq]
```

```python
import functools

import jax
import jax.numpy as jnp
from jax import lax
from jax.experimental import pallas as pl
from jax.experimental.pallas import tpu as pltpu

D_MODEL = 1024
D_HEAD = 64
N_HEADS = 8
N_PAIRS = N_HEADS // 2
D_IDX = 32
N_META = 16
K_SEL_MAX = 256
D_FF = 4 * D_MODEL
ROPE_THETA = 500000.0
RMS_EPS = 1e-6

LANES = 128
BLK = 256
SEQ_ALIGN = 512
VMEM_LIMIT = 60 * 1024 * 1024

EXP2_ZERO_BELOW = -151.0
LOG2_E = 1.4426950408889634
NEG_INF = float("-inf")
INT_MIN = -2 ** 31
KEY_OF_NEG_INF = 0x007FFFFF
CELL_SETS = 8
CELL_DEPTH = 4
SUM_SAFE_LO = 1e-30
SUM_SAFE_HI = 1e30

F32 = jnp.float32
BF16 = jnp.bfloat16


def _dot(a, b):
    return jnp.dot(a, b, preferred_element_type=F32)


def _dot_nt(a, b):
    return lax.dot_general(a, b, (((1,), (1,)), ((), ())), preferred_element_type=F32)


def _rms(x, g):
    return x * lax.rsqrt(jnp.mean(x * x, axis=-1, keepdims=True) + RMS_EPS) * g


def _params(n_axes=1):
    return pltpu.CompilerParams(
        dimension_semantics=("arbitrary",) * n_axes, vmem_limit_bytes=VMEM_LIMIT)


def _resident():
    return pl.BlockSpec(memory_space=pltpu.VMEM)


def _prenorm_kernel(h_ref, g_ref, o_ref):
    o_ref[...] = _rms(h_ref[...], g_ref[...]).astype(BF16)


def _prenorm(h, g):
    tp = h.shape[0]
    tm = SEQ_ALIGN
    return pl.pallas_call(
        _prenorm_kernel,
        grid=(tp // tm,),
        in_specs=[pl.BlockSpec((tm, D_MODEL), lambda i: (i, 0)),
                  pl.BlockSpec((1, D_MODEL), lambda i: (0, 0))],
        out_specs=pl.BlockSpec((tm, D_MODEL), lambda i: (i, 0)),
        out_shape=jax.ShapeDtypeStruct((tp, D_MODEL), BF16),
        compiler_params=_params(),
        name="prenorm",
    )(h, g)


def _proj_rows_kernel(x_ref, wp_ref, wr_ref, wrot_ref, cos_ref, sin_ref, wg_ref, bg_ref,
                      qsb_ref, vsb_ref, vds_ref, qds_ref, kix_ref, gate_ref):
    x = x_ref[...]
    y = _dot(x, wp_ref[...])
    for n, ref in enumerate((qsb_ref, vsb_ref, vds_ref)):
        for g in range(N_PAIRS):
            c = (n * N_PAIRS + g) * LANES
            ref[g] = y[:, c:c + LANES].astype(BF16)
    a = _dot(x, wr_ref[...])
    b = _dot(x, wrot_ref[...])
    cos, sin = cos_ref[...], sin_ref[...]
    cds, sds = cos[:, :LANES], sin[:, :LANES]
    cix, six = cos[:, LANES:], sin[:, LANES:]

    def rope(g, c, s):
        sl = slice(g * LANES, (g + 1) * LANES)
        return (a[:, sl] * c + b[:, sl] * s).astype(BF16)

    for g in range(N_PAIRS):
        qds_ref[g] = rope(g, cds, sds)
    kix_ref[...] = rope(N_PAIRS, cix, six)
    gate_ref[...] = jax.nn.sigmoid(_dot(x, wg_ref[...]) + bg_ref[...]).astype(BF16)


def _proj_rows(hn, wp, wr, wrot, cos, sin, wg, bg):
    tp = hn.shape[0]
    tm = BLK
    full = lambda a: pl.BlockSpec(a.shape, lambda i: (0,) * a.ndim)
    grp = pl.BlockSpec((N_PAIRS, tm, LANES), lambda i: (0, i, 0))
    grp_shape = jax.ShapeDtypeStruct((N_PAIRS, tp, LANES), BF16)
    return pl.pallas_call(
        _proj_rows_kernel,
        grid=(tp // tm,),
        in_specs=[pl.BlockSpec((tm, D_MODEL), lambda i: (i, 0)), full(wp), full(wr), full(wrot),
                  pl.BlockSpec((tm, 2 * LANES), lambda i: (i, 0)),
                  pl.BlockSpec((tm, 2 * LANES), lambda i: (i, 0)),
                  full(wg), full(bg)],
        out_specs=[grp, grp, grp, grp,
                   pl.BlockSpec((tm, LANES), lambda i: (i, 0)),
                   pl.BlockSpec((tm, 2 * D_MODEL), lambda i: (i, 0))],
        out_shape=[grp_shape, grp_shape, grp_shape, grp_shape,
                   jax.ShapeDtypeStruct((tp, LANES), BF16),
                   jax.ShapeDtypeStruct((tp, 2 * D_MODEL), BF16)],
        compiler_params=_params(),
        name="proj_rows",
    )(hn, wp, wr, wrot, cos, sin, wg, bg)


def _proj_cols_kernel(x_ref, wp_ref, wr_ref, wrot_ref, cos_ref, sin_ref, ww_ref,
                      ksb_ref, kds_ref, qix_ref, wix_ref):
    x = x_ref[...]
    y = _dot_nt(wp_ref[...], x)
    for g in range(N_PAIRS):
        ksb_ref[g, 0] = y[g * LANES:(g + 1) * LANES, :].astype(BF16)
    a = _dot_nt(wr_ref[...], x)
    b = _dot_nt(wrot_ref[...], x)
    cos, sin = cos_ref[...], sin_ref[...]
    cds, sds = cos[:LANES, :], sin[:LANES, :]
    cix, six = cos[LANES:, :], sin[LANES:, :]

    def rope(g, c, s):
        sl = slice(g * LANES, (g + 1) * LANES)
        return (a[sl, :] * c + b[sl, :] * s).astype(BF16)

    for g in range(N_PAIRS):
        kds_ref[g, 0] = rope(g, cds, sds)
    for g in range(2):
        qix_ref[g] = rope(N_PAIRS + g, cix, six)
    wix_ref[...] = _dot_nt(ww_ref[...], x)


def _proj_cols(hn, wp, wr, wrot, cos, sin, ww):
    tp = hn.shape[0]
    nkb = tp // BLK
    full = lambda a: pl.BlockSpec(a.shape, lambda i: (0,) * a.ndim)
    kblk = pl.BlockSpec((N_PAIRS, 1, LANES, BLK), lambda i: (0, i, 0, 0))
    return pl.pallas_call(
        _proj_cols_kernel,
        grid=(nkb,),
        in_specs=[pl.BlockSpec((BLK, D_MODEL), lambda i: (i, 0)), full(wp), full(wr), full(wrot),
                  pl.BlockSpec((2 * LANES, BLK), lambda i: (0, i)),
                  pl.BlockSpec((2 * LANES, BLK), lambda i: (0, i)),
                  full(ww)],
        out_specs=[kblk, kblk,
                   pl.BlockSpec((2, LANES, BLK), lambda i: (0, 0, i)),
                   pl.BlockSpec((LANES, BLK), lambda i: (0, i))],
        out_shape=[jax.ShapeDtypeStruct((N_PAIRS, nkb, LANES, BLK), BF16),
                   jax.ShapeDtypeStruct((N_PAIRS, nkb, LANES, BLK), BF16),
                   jax.ShapeDtypeStruct((2, LANES, tp), BF16),
                   jax.ShapeDtypeStruct((LANES, tp), F32)],
        compiler_params=_params(),
        name="proj_cols",
    )(hn, wp, wr, wrot, cos, sin, ww)


def _split_pair_queries(q_ref, qm_scr):
    lane = lax.broadcasted_iota(jnp.int32, (BLK, LANES), 1)
    for j in range(N_PAIRS):
        qp = q_ref[j]
        zero = jnp.zeros_like(qp)
        qm_scr[2 * j] = jnp.where(lane < D_HEAD, qp, zero)
        qm_scr[2 * j + 1] = jnp.where(lane >= D_HEAD, qp, zero)


def _sb_kernel(q_ref, kt_ref, v_ref, u_ref, o_ref, qm_scr, later_scr, acc_scr):
    i = pl.program_id(0)
    _split_pair_queries(q_ref, qm_scr)
    acc_scr[...] = jnp.zeros_like(acc_scr)
    later_scr[...] = jnp.zeros_like(later_scr)
    row = lax.broadcasted_iota(jnp.int32, (BLK, BLK), 0)
    col = lax.broadcasted_iota(jnp.int32, (BLK, BLK), 1)
    visible_diag = col < row
    lane = lax.broadcasted_iota(jnp.int32, (BLK, LANES), 1)
    first_half = lane < D_HEAD
    u = u_ref[...]

    def block(kb, diag):
        start = pl.multiple_of(kb * BLK, BLK)
        worst = None
        for j in range(N_PAIRS):
            kt = kt_ref[j, kb]
            vv = v_ref[j, pl.ds(start, BLK), :]
            pvs = []
            for h in (2 * j, 2 * j + 1):
                z = _dot(qm_scr[h], kt)
                sp = jnp.maximum(z, 0.0) + jnp.log2(1.0 + jnp.exp2(-jnp.abs(z)))
                log_not = -sp
                if diag:
                    log_not = jnp.where(visible_diag, log_not, 0.0)
                inside = _dot(log_not.astype(BF16), u)
                later = later_scr[h]
                a = jnp.exp2(z - sp + inside + jnp.concatenate([later, later], axis=1))
                if diag:
                    a = jnp.where(visible_diag, a, 0.0)
                pvs.append(_dot(a.astype(BF16), vv))
                later = later + jnp.broadcast_to(inside[:, 0:1] + log_not[:, 0:1], (BLK, LANES))
                later_scr[h] = later
                worst = later if worst is None else jnp.maximum(worst, later)
            acc_scr[j] += jnp.where(first_half, pvs[0], pvs[1])
        return jnp.max(worst)

    def cond(carry):
        kb, worst = carry
        return jnp.logical_and(kb >= 0, worst > EXP2_ZERO_BELOW)

    def body(carry):
        kb, _ = carry
        return kb - 1, block(kb, False)

    lax.while_loop(cond, body, (i - 1, block(i, True)))

    for j in range(N_PAIRS):
        o_ref[:, j * LANES:(j + 1) * LANES] = acc_scr[j].astype(BF16)


def _sb_attention(q, kt, v, u):
    tp = q.shape[1]
    return pl.pallas_call(
        _sb_kernel,
        grid=(tp // BLK,),
        in_specs=[pl.BlockSpec((N_PAIRS, BLK, LANES), lambda i: (0, i, 0)),
                  _resident(), _resident(), _resident()],
        out_specs=pl.BlockSpec((BLK, N_PAIRS * LANES), lambda i: (i, 0)),
        out_shape=jax.ShapeDtypeStruct((tp, N_PAIRS * LANES), BF16),
        scratch_shapes=[pltpu.VMEM((N_HEADS, BLK, LANES), BF16),
                        pltpu.VMEM((N_HEADS, BLK, LANES), F32),
                        pltpu.VMEM((N_PAIRS, BLK, LANES), F32)],
        compiler_params=_params(),
        name="sb_attention",
    )(q, kt, v, u)


def _key_to_float(cand):
    neg_top = cand < 0
    bits = jnp.where(neg_top, cand ^ jnp.int32(INT_MIN), ~cand)
    f = lax.bitcast_convert_type(bits, F32)
    return jnp.where(jnp.logical_or(neg_top, cand > KEY_OF_NEG_INF), f, NEG_INF)


def _key16_to_float(p):
    top = p >= 0x8000
    pattern = jnp.where(top, p ^ 0x8000, p ^ 0xFFFF)
    f = lax.bitcast_convert_type(jnp.left_shift(pattern, 16), F32)
    return jnp.where(jnp.logical_or(top, p > (KEY_OF_NEG_INF >> 16)), f, NEG_INF)


def _select_kernel(kix_ref, qixt_ref, wixt_ref, mask_ref, s_scr, shi_scr, rhs_scr, cell_scr, jst_scr,
                   thr_scr, cgt_scr, ceq_scr, *, k_top, t_real):
    i = pl.program_id(0)
    nblk = i + 1
    nkb = mask_ref.shape[1]
    sub = lax.broadcasted_iota(jnp.int32, (LANES, BLK), 0)
    for h in range(N_HEADS):
        src = qixt_ref[h // 4]
        rhs_scr[h] = jnp.where(sub // D_IDX == h % 4, src, jnp.zeros_like(src))
    w = wixt_ref[0:N_HEADS, :]
    srow = lax.broadcasted_iota(jnp.int32, (BLK, BLK), 0)
    tq = i * BLK + lax.broadcasted_iota(jnp.int32, (BLK, BLK), 1)

    def fill(pair, carry):
        for half in range(2):
            start = pl.multiple_of((2 * pair + half) * BLK, BLK)
            kx = kix_ref[pl.ds(start, BLK), :]
            sc = None
            for h in range(N_HEADS):
                term = w[h:h + 1, :] * jnp.maximum(_dot(kx, rhs_scr[h]), 0.0)
                sc = term if sc is None else sc + term
            sc = jnp.where(start + srow <= tq, sc, NEG_INF)
            s_scr[pl.ds(start, BLK), :] = sc
            shi_scr[pl.ds(start, BLK), :] = sc.astype(BF16)
        return carry

    lax.fori_loop(0, (nblk + 1) // 2, fill, 0)

    def reduce_blocks(fn, init, combine):
        def body(kb, acc):
            start = pl.multiple_of(kb * BLK, BLK)
            blk = s_scr[pl.ds(start, BLK), :].reshape(BLK // 8, 8, BLK)
            return combine(acc, fn(blk, start))
        return lax.fori_loop(0, nblk, body, init)

    def fold4(m):
        return m.reshape(8, 4, 8, BLK).sum(axis=0)

    def total(acc):
        t = acc.sum(axis=0).sum(axis=0, keepdims=True)
        return jnp.broadcast_to(t, (8, BLK))

    zeros4 = jnp.zeros((4, 8, BLK), F32)
    add = lambda a, b: a + b

    def count(pred):
        return total(reduce_blocks(lambda blk, st: fold4(jnp.where(pred(blk, st), 1.0, 0.0)), zeros4, add))

    k_f = jnp.float32(k_top)

    one_b, zero_b = jnp.ones((), BF16), jnp.zeros((), BF16)

    def count_upper(thr_b):
        def body(kb, acc):
            start = pl.multiple_of(kb * BLK, BLK)
            blk = shi_scr[pl.ds(start, BLK), :].reshape(BLK // 16, 16, BLK)
            m = jnp.where(blk >= thr_b[None], one_b, zero_b)
            parts = [m[r] for r in range(BLK // 16)]
            while len(parts) > 1:
                parts = [parts[n] + parts[n + 1] for n in range(0, len(parts), 2)]
            return acc + parts[0].astype(F32)
        acc = lax.fori_loop(0, nblk, body, jnp.zeros((16, BLK), F32))
        return jnp.broadcast_to(acc.sum(axis=0, keepdims=True), (8, BLK))

    def bisect_upper(it, p):
        cand = p | lax.shift_left(jnp.int32(1), 15 - it)
        thr_b = jnp.broadcast_to(_key16_to_float(cand)[0:1, :], (16, BLK)).astype(BF16)
        return jnp.where(count_upper(thr_b) >= k_f, cand, p)

    p1 = lax.fori_loop(0, 16, bisect_upper, jnp.zeros((8, BLK), jnp.int32))
    edges = [_key16_to_float(p1 + d) for d in (-1, 0, 1)]
    counts4 = reduce_blocks(
        lambda blk, st: jnp.stack([fold4(jnp.where(blk >= e[None], 1.0, 0.0)) for e in edges]
                                  + [fold4(jnp.where(blk == 0.0, 1.0, 0.0))]),
        jnp.zeros((4, 4, 8, BLK), F32), add)
    c_lo, c_mid, c_hi, n_zero = (total(counts4[n]) for n in range(4))
    lower = c_mid < k_f
    p = jnp.where(lower, p1 - 1, p1)
    cnt_at_p = jnp.where(lower, c_lo, c_mid)
    cnt_above = jnp.where(lower, c_mid, c_hi)
    bracketed = (c_lo >= k_f) & (c_hi < k_f)
    tlane = i * BLK + lax.broadcasted_iota(jnp.int32, (8, BLK), 1)
    open_row = tlane + 1 < k_top
    bucket_lo = _key16_to_float(p)
    bucket_hi = _key16_to_float(p + 1)
    key_lo = jnp.where(p >= 0x8000, jnp.left_shift(p, 16), jnp.left_shift(p, 16) | 0xFFFF)
    bucket_size = cnt_at_p - cnt_above
    need_in = k_f - cnt_above

    zeros_in = jnp.where((bucket_lo <= 0.0) & (bucket_hi > 0.0), n_zero, 0.0)

    cell_scr[...] = jnp.full_like(cell_scr, NEG_INF)
    per_set = BLK // 8 // CELL_SETS

    def collect(kb, carry):
        start = pl.multiple_of(kb * BLK, BLK)
        blk = s_scr[pl.ds(start, BLK), :].reshape(per_set, CELL_SETS, 8, BLK)
        inside = (blk >= bucket_lo[None, None]) & (blk < bucket_hi[None, None]) & (blk != 0.0)
        x = jnp.where(inside, blk, NEG_INF)
        for s in range(CELL_SETS):
            best = [cell_scr[d, s] for d in range(CELL_DEPTH)]
            for a in range(per_set):
                v = x[a, s]
                for d in range(CELL_DEPTH - 1):
                    best[d], v = jnp.maximum(best[d], v), jnp.minimum(best[d], v)
                best[-1] = jnp.maximum(best[-1], v)
            for d in range(CELL_DEPTH):
                cell_scr[d, s] = best[d]
        return carry

    lax.fori_loop(0, nblk, collect, 0)
    cells = cell_scr[...].reshape(CELL_DEPTH * CELL_SETS, 8, BLK)

    def count_cells(pred):
        t = jnp.where(pred, 1.0, 0.0).sum(axis=0).sum(axis=0, keepdims=True)
        return jnp.broadcast_to(t, (8, BLK))

    def count_bucket(rel):
        return lambda t: count_cells(rel(cells, t[None])) + jnp.where(rel(0.0, t), zeros_in, 0.0)

    ge, gt, eq = (lambda a, b: a >= b), (lambda a, b: a > b), (lambda a, b: a == b)
    kept = count_cells(cells > NEG_INF) + zeros_in
    redo = jnp.logical_not(open_row) & ((kept < bucket_size) | jnp.logical_not(bracketed))

    def bisect_cells(it, low):
        cand = low | lax.shift_left(jnp.int32(1), 15 - it)
        cnt = count_bucket(ge)(_key_to_float(key_lo + cand))
        return jnp.where(cnt >= need_in, cand, low)

    def key_float(key):
        return _key_to_float(jnp.where(open_row, jnp.int32(KEY_OF_NEG_INF), key))

    thr0 = key_float(key_lo + lax.fori_loop(0, 16, bisect_cells, jnp.zeros((8, BLK), jnp.int32)))
    thr_c = jnp.where(cells >= thr0[None], cells, jnp.inf).min(axis=0).min(axis=0, keepdims=True)
    thr_c = jnp.broadcast_to(thr_c, (8, BLK))
    thr_c = jnp.where((zeros_in > 0.0) & (thr0 <= 0.0), jnp.minimum(thr_c, 0.0), thr_c)
    thr_scr[...] = thr_c
    cgt_scr[...] = cnt_above + count_bucket(gt)(thr_c)
    ceq_scr[...] = count_bucket(eq)(thr_c)

    @pl.when(jnp.max(jnp.where(redo, 1.0, 0.0)) > 0.0)
    def _():
        def bisect_all(it, c):
            cand = c | lax.shift_left(jnp.int32(1), 31 - it)
            t = _key_to_float(cand)
            cnt = count(lambda blk, st: blk >= t[None])
            return jnp.where(cnt >= k_f, cand, c)

        t0 = key_float(lax.fori_loop(0, 32, bisect_all, jnp.zeros((8, BLK), jnp.int32)))
        mins = reduce_blocks(
            lambda blk, st: jnp.where(blk >= t0[None], blk, jnp.inf).reshape(8, 4, 8, BLK).min(axis=0),
            jnp.full((4, 8, BLK), jnp.inf, F32), jnp.minimum)
        t = jnp.broadcast_to(mins.min(axis=0).min(axis=0, keepdims=True), (8, BLK))
        both = reduce_blocks(
            lambda blk, st: jnp.stack([fold4(jnp.where(blk > t[None], 1.0, 0.0)),
                                       fold4(jnp.where(blk == t[None], 1.0, 0.0))]),
            jnp.zeros((2, 4, 8, BLK), F32), add)
        thr_scr[...] = t
        cgt_scr[...] = total(both[0])
        ceq_scr[...] = total(both[1])

    thr, c_gt, c_eq = thr_scr[...], cgt_scr[...], ceq_scr[...]
    need = k_f - c_gt
    finite_thr = thr > NEG_INF
    ambiguous = (c_eq > need) & finite_thr & (tlane < t_real)
    jst_scr[...] = jnp.where(finite_thr, jnp.int32(2 ** 30), jnp.int32(-1))

    @pl.when(jnp.max(jnp.where(ambiguous, 1.0, 0.0)) > 0.0)
    def _():
        sidx = lax.broadcasted_iota(jnp.int32, (BLK // 8, 8, BLK), 0) * 8 + \
            lax.broadcasted_iota(jnp.int32, (BLK // 8, 8, BLK), 1)

        def bisect_idx(it, jc):
            cand = jc | lax.shift_left(jnp.int32(1), 14 - it)
            cnt = count(lambda blk, st: (blk == thr[None]) & (st + sidx < cand[None]))
            return jnp.where(cnt < need, cand, jc)

        jc = lax.fori_loop(0, 15, bisect_idx, jnp.zeros((8, BLK), jnp.int32))
        jst_scr[...] = jnp.where(ambiguous, jc, jst_scr[...])

    thr_row = thr[0:1, :]
    jst_row = jst_scr[0:1, :]

    def emit(kb, carry):
        start = pl.multiple_of(kb * BLK, BLK)
        blk = s_scr[pl.ds(start, BLK), :]
        sel = (blk > thr_row) | ((blk == thr_row) & (start + srow <= jst_row))
        mask_ref[0, kb] = jnp.where(sel, 0.0, NEG_INF).T.astype(BF16)
        return carry

    lax.fori_loop(0, nblk, emit, 0)

    def blank(kb, carry):
        mask_ref[0, kb] = jnp.full((BLK, BLK), NEG_INF, BF16)
        return carry

    lax.fori_loop(nblk, nkb, blank, 0)


def _select(kix, qixt, wixt, k_top, t_real):
    tp = kix.shape[0]
    assert tp < 2 ** 15
    nkb = tp // BLK
    kern = functools.partial(_select_kernel, k_top=k_top, t_real=t_real)
    return pl.pallas_call(
        kern,
        grid=(nkb,),
        in_specs=[_resident(),
                  pl.BlockSpec((2, LANES, BLK), lambda i: (0, 0, i)),
                  pl.BlockSpec((LANES, BLK), lambda i: (0, i))],
        out_specs=pl.BlockSpec((1, nkb, BLK, BLK), lambda i: (i, 0, 0, 0)),
        out_shape=jax.ShapeDtypeStruct((nkb, nkb, BLK, BLK), BF16),
        scratch_shapes=[pltpu.VMEM((tp, BLK), F32),
                        pltpu.VMEM((tp, BLK), BF16),
                        pltpu.VMEM((N_HEADS, LANES, BLK), BF16),
                        pltpu.VMEM((CELL_DEPTH, CELL_SETS, 8, BLK), F32),
                        pltpu.VMEM((8, BLK), jnp.int32),
                        pltpu.VMEM((8, BLK), F32),
                        pltpu.VMEM((8, BLK), F32),
                        pltpu.VMEM((8, BLK), F32)],
        compiler_params=_params(),
        name="select_mask",
    )(kix, qixt, wixt)


def _dsa_kernel(q_ref, kt_ref, v_ref, mask_ref, o_ref, qm_scr, shift_scr, lsum_scr, acc_scr):
    i = pl.program_id(0)
    _split_pair_queries(q_ref, qm_scr)
    lane = lax.broadcasted_iota(jnp.int32, (BLK, LANES), 1)
    first_half = lane < D_HEAD

    def logits(h, kt, bias):
        return _dot(qm_scr[h], kt) + bias

    def accumulate(shifted):
        lsum_scr[...] = jnp.zeros_like(lsum_scr)
        acc_scr[...] = jnp.zeros_like(acc_scr)

        def body(pair, carry):
            for kb in (2 * pair, 2 * pair + 1):
                bias = mask_ref[0, kb].astype(F32)
                start = pl.multiple_of(kb * BLK, BLK)
                for j in range(N_PAIRS):
                    kt = kt_ref[j, kb]
                    vv = v_ref[j, pl.ds(start, BLK), :]
                    pvs = []
                    for h in (2 * j, 2 * j + 1):
                        s = logits(h, kt, bias)
                        if shifted:
                            m = shift_scr[h]
                            s = s - jnp.concatenate([m, m], axis=1)
                        p = jnp.exp(s)
                        lsum_scr[h] += p[:, :LANES] + p[:, LANES:]
                        pvs.append(_dot(p.astype(BF16), vv))
                    acc_scr[j] += jnp.where(first_half, pvs[0], pvs[1])
            return carry

        lax.fori_loop(0, (i + 2) // 2, body, 0)

    def row_sums():
        return [jnp.sum(lsum_scr[h], axis=1, keepdims=True) for h in range(N_HEADS)]

    accumulate(False)
    unsafe = jnp.zeros((BLK, 1), F32)
    for l in row_sums():
        unsafe = jnp.maximum(unsafe, jnp.where((l > SUM_SAFE_LO) & (l < SUM_SAFE_HI), 0.0, 1.0))
    acc_big = jnp.where(jnp.abs(acc_scr[...]) < SUM_SAFE_HI, 0.0, 1.0)

    @pl.when((jnp.max(unsafe) > 0.0) | (jnp.max(acc_big) > 0.0))
    def _():
        shift_scr[...] = jnp.full_like(shift_scr, NEG_INF)

        def body(kb, carry):
            bias = mask_ref[0, kb].astype(F32)
            for j in range(N_PAIRS):
                kt = kt_ref[j, kb]
                for h in (2 * j, 2 * j + 1):
                    s = logits(h, kt, bias)
                    shift_scr[h] = jnp.maximum(shift_scr[h], jnp.maximum(s[:, :LANES], s[:, LANES:]))
            return carry

        lax.fori_loop(0, i + 1, body, 0)
        for h in range(N_HEADS):
            m = jnp.max(shift_scr[h], axis=1, keepdims=True)
            m = jnp.where(m == NEG_INF, 0.0, m)
            shift_scr[h] = jnp.broadcast_to(m, (BLK, LANES))
        accumulate(True)

    sums = row_sums()
    for j in range(N_PAIRS):
        denom = jnp.where(first_half, sums[2 * j], sums[2 * j + 1])
        o_ref[:, j * LANES:(j + 1) * LANES] = (acc_scr[j] / denom).astype(BF16)


def _dsa_attention(q, kt, v, mask):
    tp = q.shape[1]
    nkb = tp // BLK
    return pl.pallas_call(
        _dsa_kernel,
        grid=(nkb,),
        in_specs=[pl.BlockSpec((N_PAIRS, BLK, LANES), lambda i: (0, i, 0)),
                  _resident(), _resident(),
                  pl.BlockSpec((1, nkb, BLK, BLK), lambda i: (i, 0, 0, 0))],
        out_specs=pl.BlockSpec((BLK, N_PAIRS * LANES), lambda i: (i, 0)),
        out_shape=jax.ShapeDtypeStruct((tp, N_PAIRS * LANES), BF16),
        scratch_shapes=[pltpu.VMEM((N_HEADS, BLK, LANES), BF16),
                        pltpu.VMEM((N_HEADS, BLK, LANES), F32),
                        pltpu.VMEM((N_HEADS, BLK, LANES), F32),
                        pltpu.VMEM((N_PAIRS, BLK, LANES), F32)],
        compiler_params=_params(),
        name="dsa_attention",
    )(q, kt, v, mask)


def _merge_kernel(osb_ref, ods_ref, gate_ref, h_ref, wsb_ref, wds_ref, wout_ref, g_ref, o_ref):
    gate = gate_ref[...].astype(F32)
    merged = (gate[:, :D_MODEL] * _dot(osb_ref[...], wsb_ref[...])
              + gate[:, D_MODEL:] * _dot(ods_ref[...], wds_ref[...]))
    mix = _dot(merged.astype(BF16), wout_ref[...])
    o_ref[...] = h_ref[...] + _rms(mix, g_ref[...])


def _merge(osb, ods, gates, h, wsb, wds, wout, g):
    tp = h.shape[0]
    tm = BLK
    full = lambda a: pl.BlockSpec(a.shape, lambda i: (0,) * a.ndim)
    rows = lambda n: pl.BlockSpec((tm, n), lambda i: (i, 0))
    return pl.pallas_call(
        _merge_kernel,
        grid=(tp // tm,),
        in_specs=[rows(N_PAIRS * LANES), rows(N_PAIRS * LANES), rows(2 * D_MODEL), rows(D_MODEL),
                  full(wsb), full(wds), full(wout), full(g)],
        out_specs=rows(D_MODEL),
        out_shape=jax.ShapeDtypeStruct((tp, D_MODEL), F32),
        compiler_params=_params(),
        name="merge_out",
    )(osb, ods, gates, h, wsb, wds, wout, g)


def _mlp_kernel(h_ref, gpre_ref, w1_ref, w2_ref, gpost_ref, o_ref):
    h = h_ref[...]
    hn = _rms(h, gpre_ref[...]).astype(BF16)
    ff = None
    for c in range(D_FF // D_MODEL):
        sl = slice(c * D_MODEL, (c + 1) * D_MODEL)
        u = jnp.maximum(_dot(hn, w1_ref[:, sl]), 0.0)
        part = _dot((u * u).astype(BF16), w2_ref[sl, :])
        ff = part if ff is None else ff + part
    o_ref[...] = h + _rms(ff, gpost_ref[...])


def _mlp(h, gpre, w1, w2, gpost):
    tp = h.shape[0]
    tm = BLK
    full = lambda a: pl.BlockSpec(a.shape, lambda i: (0,) * a.ndim)
    rows = pl.BlockSpec((tm, D_MODEL), lambda i: (i, 0))
    return pl.pallas_call(
        _mlp_kernel,
        grid=(tp // tm,),
        in_specs=[rows, full(gpre), full(w1), full(w2), full(gpost)],
        out_specs=rows,
        out_shape=jax.ShapeDtypeStruct((tp, D_MODEL), F32),
        compiler_params=_params(),
        name="mlp",
    )(h, gpre, w1, w2, gpost)


def _rotate_half_cols(w, d_head):
    rot = d_head // 4
    half = rot // 2
    n = w.shape[1] // d_head
    w3 = w.reshape(w.shape[0], n, d_head)
    out = jnp.zeros_like(w3)
    out = out.at[:, :, :half].set(-w3[:, :, half:rot])
    out = out.at[:, :, half:rot].set(w3[:, :, :half])
    return out.reshape(w.shape)


def _rope_tables(tp, d_head):
    rot = d_head // 4
    half = rot // 2
    inv_freq = jnp.power(jnp.float32(ROPE_THETA), -jnp.arange(half, dtype=F32) * (2.0 / rot))
    ang = jnp.arange(tp).astype(F32)[:, None] * inv_freq[None, :]
    cos, sin = jnp.cos(ang), jnp.sin(ang)
    ones = jnp.ones((tp, d_head - rot), F32)
    cos_h = jnp.concatenate([cos, cos, ones], axis=1)
    sin_h = jnp.concatenate([sin, sin, jnp.zeros_like(ones)], axis=1)
    reps = LANES // d_head
    return jnp.tile(cos_h, (1, reps)), jnp.tile(sin_h, (1, reps))


def kernel(x, meta_tokens, w_in, b_gate, w_branch_sb, w_branch_dsa, w_out, g_mix_pre, g_mix_post,
           w_mlp_in, w_mlp_out, g_mlp_pre, g_mlp_post):
    b, seq, d = x.shape
    assert b == 1 and d == D_MODEL and w_in.shape[0] == 1
    k_top = min(K_SEL_MAX, seq // 4)
    t_real = seq + N_META
    tp = -(-t_real // SEQ_ALIGN) * SEQ_ALIGN
    h = jnp.concatenate([meta_tokens.astype(x.dtype), x[0],
                         jnp.zeros((tp - t_real, d), x.dtype)], axis=0)

    w = w_in[0]
    hw = N_HEADS * D_HEAD
    iq = N_HEADS * D_IDX
    o = 0
    w_qsb, w_ksb, w_vsb = w[:, o:o + hw], w[:, o + hw:o + 2 * hw], w[:, o + 2 * hw:o + 3 * hw]
    o += 3 * hw
    w_qds, w_kds, w_vds = w[:, o:o + hw], w[:, o + hw:o + 2 * hw], w[:, o + 2 * hw:o + 3 * hw]
    o += 3 * hw
    w_qix, w_kix, w_wix = w[:, o:o + iq], w[:, o + iq:o + iq + D_IDX], w[:, o + iq + D_IDX:o + iq + D_IDX + N_HEADS]
    o += iq + D_IDX + N_HEADS
    w_gate = w[:, o:o + 2 * D_MODEL]
    w_qsb = w_qsb * (D_HEAD ** -0.5 * LOG2_E)
    w_qds = w_qds * (D_HEAD ** -0.5)
    w_wix = jnp.pad(w_wix * ((D_IDX * N_HEADS) ** -0.5), ((0, 0), (0, LANES - N_HEADS)))
    w_kix4 = jnp.tile(w_kix, (1, LANES // D_IDX))

    cos_ds, sin_ds = _rope_tables(tp, D_HEAD)
    cos_ix, sin_ix = _rope_tables(tp, D_IDX)
    cos = jnp.concatenate([cos_ds, cos_ix], axis=1)
    sin = jnp.concatenate([sin_ds, sin_ix], axis=1)

    bf = lambda a: a.astype(BF16)
    wp_r = bf(jnp.concatenate([w_qsb, w_vsb, w_vds], axis=1))
    wr_r = jnp.concatenate([w_qds, w_kix4], axis=1)
    wrot_r = jnp.concatenate([_rotate_half_cols(w_qds, D_HEAD), _rotate_half_cols(w_kix4, D_IDX)], axis=1)
    wp_c = bf(w_ksb.T)
    wr_c = jnp.concatenate([w_kds, w_qix], axis=1)
    wrot_c = jnp.concatenate([_rotate_half_cols(w_kds, D_HEAD), _rotate_half_cols(w_qix, D_IDX)], axis=1)

    hn = _prenorm(h, g_mix_pre)
    q_sb, v_sb, v_ds, q_ds, k_ix4, gates = _proj_rows(
        hn, wp_r, bf(wr_r), bf(wrot_r), cos, sin, bf(w_gate), b_gate)
    kt_sb, kt_ds, q_ixt, w_ixt = _proj_cols(
        hn, wp_c, bf(wr_c.T), bf(wrot_c.T), cos.T, sin.T, bf(w_wix.T))

    later_keys = (jnp.arange(BLK)[:, None] > jnp.arange(BLK)[None, :]).astype(BF16)
    o_sb = _sb_attention(q_sb, kt_sb, v_sb, later_keys)

    mask = _select(k_ix4, q_ixt, w_ixt, k_top, t_real)
    o_ds = _dsa_attention(q_ds, kt_ds, v_ds, mask)

    h1 = _merge(o_sb, o_ds, gates, h, bf(w_branch_sb[0]), bf(w_branch_dsa[0]), bf(w_out[0]), g_mix_post)
    h2 = _mlp(h1, g_mlp_pre, bf(w_mlp_in[0]), bf(w_mlp_out[0]), g_mlp_post)
    return h2[None, N_META:N_META + seq]
```

```python
import functools

import jax
import jax.numpy as jnp
from jax import lax
from jax.experimental import pallas as pl
from jax.experimental.pallas import tpu as pltpu

D_MODEL = 1024
D_HEAD = 64
N_HEADS = 8
N_PAIRS = N_HEADS // 2
D_IDX = 32
N_META = 16
K_SEL_MAX = 256
D_FF = 4 * D_MODEL
ROPE_THETA = 500000.0
RMS_EPS = 1e-6

LANES = 128
BLK = 256
SEQ_ALIGN = 512
VMEM_LIMIT = 60 * 1024 * 1024

EXP2_ZERO_BELOW = -151.0
LOG2_E = 1.4426950408889634
NEG_INF = float("-inf")
INT_MIN = -2 ** 31
KEY_OF_NEG_INF = 0x007FFFFF
CELL_SETS = 8
CELL_DEPTH = 4
COUNT_GROUP = 4
SUM_SAFE_LO = 1e-30
SUM_SAFE_HI = 1e30

F32 = jnp.float32
BF16 = jnp.bfloat16


def _dot(a, b):
    return jnp.dot(a, b, preferred_element_type=F32)


def _dot_nt(a, b):
    return lax.dot_general(a, b, (((1,), (1,)), ((), ())), preferred_element_type=F32)


def _rms(x, g):
    return x * lax.rsqrt(jnp.mean(x * x, axis=-1, keepdims=True) + RMS_EPS) * g


def _params(n_axes=1):
    return pltpu.CompilerParams(
        dimension_semantics=("arbitrary",) * n_axes, vmem_limit_bytes=VMEM_LIMIT)


def _resident():
    return pl.BlockSpec(memory_space=pltpu.VMEM)


def _prenorm_kernel(h_ref, g_ref, o_ref):
    o_ref[...] = _rms(h_ref[...], g_ref[...]).astype(BF16)


def _prenorm(h, g):
    tp = h.shape[0]
    tm = SEQ_ALIGN
    return pl.pallas_call(
        _prenorm_kernel,
        grid=(tp // tm,),
        in_specs=[pl.BlockSpec((tm, D_MODEL), lambda i: (i, 0)),
                  pl.BlockSpec((1, D_MODEL), lambda i: (0, 0))],
        out_specs=pl.BlockSpec((tm, D_MODEL), lambda i: (i, 0)),
        out_shape=jax.ShapeDtypeStruct((tp, D_MODEL), BF16),
        compiler_params=_params(),
        name="prenorm",
    )(h, g)


def _proj_rows_kernel(x_ref, wp_ref, wr_ref, wrot_ref, cos_ref, sin_ref, wg_ref, bg_ref,
                      qsb_ref, vsb_ref, vds_ref, qds_ref, kix_ref, gate_ref):
    x = x_ref[...]
    y = _dot(x, wp_ref[...])
    for n, ref in enumerate((qsb_ref, vsb_ref, vds_ref)):
        for g in range(N_PAIRS):
            c = (n * N_PAIRS + g) * LANES
            ref[g] = y[:, c:c + LANES].astype(BF16)
    a = _dot(x, wr_ref[...])
    b = _dot(x, wrot_ref[...])
    cos, sin = cos_ref[...], sin_ref[...]
    cds, sds = cos[:, :LANES], sin[:, :LANES]
    cix, six = cos[:, LANES:], sin[:, LANES:]

    def rope(g, c, s):
        sl = slice(g * LANES, (g + 1) * LANES)
        return (a[:, sl] * c + b[:, sl] * s).astype(BF16)

    for g in range(N_PAIRS):
        qds_ref[g] = rope(g, cds, sds)
    kix_ref[...] = rope(N_PAIRS, cix, six)
    gate_ref[...] = jax.nn.sigmoid(_dot(x, wg_ref[...]) + bg_ref[...]).astype(BF16)


def _proj_rows(hn, wp, wr, wrot, cos, sin, wg, bg):
    tp = hn.shape[0]
    tm = BLK
    full = lambda a: pl.BlockSpec(a.shape, lambda i: (0,) * a.ndim)
    grp = pl.BlockSpec((N_PAIRS, tm, LANES), lambda i: (0, i, 0))
    grp_shape = jax.ShapeDtypeStruct((N_PAIRS, tp, LANES), BF16)
    return pl.pallas_call(
        _proj_rows_kernel,
        grid=(tp // tm,),
        in_specs=[pl.BlockSpec((tm, D_MODEL), lambda i: (i, 0)), full(wp), full(wr), full(wrot),
                  pl.BlockSpec((tm, 2 * LANES), lambda i: (i, 0)),
                  pl.BlockSpec((tm, 2 * LANES), lambda i: (i, 0)),
                  full(wg), full(bg)],
        out_specs=[grp, grp, grp, grp,
                   pl.BlockSpec((tm, LANES), lambda i: (i, 0)),
                   pl.BlockSpec((tm, 2 * D_MODEL), lambda i: (i, 0))],
        out_shape=[grp_shape, grp_shape, grp_shape, grp_shape,
                   jax.ShapeDtypeStruct((tp, LANES), BF16),
                   jax.ShapeDtypeStruct((tp, 2 * D_MODEL), BF16)],
        compiler_params=_params(),
        name="proj_rows",
    )(hn, wp, wr, wrot, cos, sin, wg, bg)


def _proj_cols_kernel(x_ref, wp_ref, wr_ref, wrot_ref, cos_ref, sin_ref, ww_ref,
                      ksb_ref, kds_ref, qix_ref, wix_ref):
    x = x_ref[...]
    y = _dot_nt(wp_ref[...], x)
    for g in range(N_PAIRS):
        ksb_ref[g, 0] = y[g * LANES:(g + 1) * LANES, :].astype(BF16)
    a = _dot_nt(wr_ref[...], x)
    b = _dot_nt(wrot_ref[...], x)
    cos, sin = cos_ref[...], sin_ref[...]
    cds, sds = cos[:LANES, :], sin[:LANES, :]
    cix, six = cos[LANES:, :], sin[LANES:, :]

    def rope(g, c, s):
        sl = slice(g * LANES, (g + 1) * LANES)
        return (a[sl, :] * c + b[sl, :] * s).astype(BF16)

    for g in range(N_PAIRS):
        kds_ref[g, 0] = rope(g, cds, sds)
    for g in range(2):
        qix_ref[g] = rope(N_PAIRS + g, cix, six)
    wix_ref[...] = _dot_nt(ww_ref[...], x)


def _proj_cols(hn, wp, wr, wrot, cos, sin, ww):
    tp = hn.shape[0]
    nkb = tp // BLK
    full = lambda a: pl.BlockSpec(a.shape, lambda i: (0,) * a.ndim)
    kblk = pl.BlockSpec((N_PAIRS, 1, LANES, BLK), lambda i: (0, i, 0, 0))
    return pl.pallas_call(
        _proj_cols_kernel,
        grid=(nkb,),
        in_specs=[pl.BlockSpec((BLK, D_MODEL), lambda i: (i, 0)), full(wp), full(wr), full(wrot),
                  pl.BlockSpec((2 * LANES, BLK), lambda i: (0, i)),
                  pl.BlockSpec((2 * LANES, BLK), lambda i: (0, i)),
                  full(ww)],
        out_specs=[kblk, kblk,
                   pl.BlockSpec((2, LANES, BLK), lambda i: (0, 0, i)),
                   pl.BlockSpec((LANES, BLK), lambda i: (0, i))],
        out_shape=[jax.ShapeDtypeStruct((N_PAIRS, nkb, LANES, BLK), BF16),
                   jax.ShapeDtypeStruct((N_PAIRS, nkb, LANES, BLK), BF16),
                   jax.ShapeDtypeStruct((2, LANES, tp), BF16),
                   jax.ShapeDtypeStruct((LANES, tp), F32)],
        compiler_params=_params(),
        name="proj_cols",
    )(hn, wp, wr, wrot, cos, sin, ww)


def _split_pair_queries(q_ref, qm_scr):
    lane = lax.broadcasted_iota(jnp.int32, (BLK, LANES), 1)
    for j in range(N_PAIRS):
        qp = q_ref[j]
        zero = jnp.zeros_like(qp)
        qm_scr[2 * j] = jnp.where(lane < D_HEAD, qp, zero)
        qm_scr[2 * j + 1] = jnp.where(lane >= D_HEAD, qp, zero)


def _sb_kernel(q_ref, kt_ref, v_ref, u_ref, o_ref, qm_scr, later_scr, acc_scr):
    i = pl.program_id(0)
    _split_pair_queries(q_ref, qm_scr)
    acc_scr[...] = jnp.zeros_like(acc_scr)
    later_scr[...] = jnp.zeros_like(later_scr)
    row = lax.broadcasted_iota(jnp.int32, (BLK, BLK), 0)
    col = lax.broadcasted_iota(jnp.int32, (BLK, BLK), 1)
    visible_diag = col < row
    lane = lax.broadcasted_iota(jnp.int32, (BLK, LANES), 1)
    first_half = lane < D_HEAD
    u = u_ref[...]

    def block(kb, diag):
        start = pl.multiple_of(kb * BLK, BLK)
        worst = None
        for j in range(N_PAIRS):
            kt = kt_ref[j, kb]
            vv = v_ref[j, pl.ds(start, BLK), :]
            pvs = []
            for h in (2 * j, 2 * j + 1):
                z = _dot(qm_scr[h], kt)
                sp = jnp.maximum(z, 0.0) + jnp.log2(1.0 + jnp.exp2(-jnp.abs(z)))
                log_not = -sp
                if diag:
                    log_not = jnp.where(visible_diag, log_not, 0.0)
                inside = _dot(log_not.astype(BF16), u)
                later = later_scr[h]
                a = jnp.exp2(z - sp + inside + jnp.concatenate([later, later], axis=1))
                if diag:
                    a = jnp.where(visible_diag, a, 0.0)
                pvs.append(_dot(a.astype(BF16), vv))
                later = later + jnp.broadcast_to(inside[:, 0:1] + log_not[:, 0:1], (BLK, LANES))
                later_scr[h] = later
                worst = later if worst is None else jnp.maximum(worst, later)
            acc_scr[j] += jnp.where(first_half, pvs[0], pvs[1])
        return jnp.max(worst)

    def cond(carry):
        kb, worst = carry
        return jnp.logical_and(kb >= 0, worst > EXP2_ZERO_BELOW)

    def body(carry):
        kb, _ = carry
        return kb - 1, block(kb, False)

    lax.while_loop(cond, body, (i - 1, block(i, True)))

    for j in range(N_PAIRS):
        o_ref[:, j * LANES:(j + 1) * LANES] = acc_scr[j].astype(BF16)


def _sb_attention(q, kt, v, u):
    tp = q.shape[1]
    return pl.pallas_call(
        _sb_kernel,
        grid=(tp // BLK,),
        in_specs=[pl.BlockSpec((N_PAIRS, BLK, LANES), lambda i: (0, i, 0)),
                  _resident(), _resident(), _resident()],
        out_specs=pl.BlockSpec((BLK, N_PAIRS * LANES), lambda i: (i, 0)),
        out_shape=jax.ShapeDtypeStruct((tp, N_PAIRS * LANES), BF16),
        scratch_shapes=[pltpu.VMEM((N_HEADS, BLK, LANES), BF16),
                        pltpu.VMEM((N_HEADS, BLK, LANES), F32),
                        pltpu.VMEM((N_PAIRS, BLK, LANES), F32)],
        compiler_params=_params(),
        name="sb_attention",
    )(q, kt, v, u)


def _key_to_float(cand):
    neg_top = cand < 0
    bits = jnp.where(neg_top, cand ^ jnp.int32(INT_MIN), ~cand)
    f = lax.bitcast_convert_type(bits, F32)
    return jnp.where(jnp.logical_or(neg_top, cand > KEY_OF_NEG_INF), f, NEG_INF)


def _key16_to_float(p):
    top = p >= 0x8000
    pattern = jnp.where(top, p ^ 0x8000, p ^ 0xFFFF)
    f = lax.bitcast_convert_type(jnp.left_shift(pattern, 16), F32)
    return jnp.where(jnp.logical_or(top, p > (KEY_OF_NEG_INF >> 16)), f, NEG_INF)


def _select_kernel(kix_ref, qixt_ref, wixt_ref, mask_ref, s_scr, shi_scr, colmax_scr, rhs_scr, cell_scr,
                   jst_scr, thr_scr, cgt_scr, ceq_scr, *, k_top, t_real):
    i = pl.program_id(0)
    nblk = i + 1
    nkb = mask_ref.shape[1]
    sub = lax.broadcasted_iota(jnp.int32, (LANES, BLK), 0)
    for h in range(N_HEADS):
        src = qixt_ref[h // 4]
        rhs_scr[h] = jnp.where(sub // D_IDX == h % 4, src, jnp.zeros_like(src))
    w = wixt_ref[0:N_HEADS, :]
    srow = lax.broadcasted_iota(jnp.int32, (BLK, BLK), 0)
    tq = i * BLK + lax.broadcasted_iota(jnp.int32, (BLK, BLK), 1)

    for extra in range(COUNT_GROUP - 1):
        start = pl.multiple_of((nblk + extra) * BLK, BLK)
        s_scr[pl.ds(start, BLK), :] = jnp.full((BLK, BLK), NEG_INF, F32)
        shi_scr[pl.ds(start, BLK), :] = jnp.full((BLK, BLK), NEG_INF, BF16)
    colmax_scr[...] = jnp.full_like(colmax_scr, NEG_INF)
    ngroups = (nblk + COUNT_GROUP - 1) // COUNT_GROUP

    def fill(pair, carry):
        for half in range(2):
            start = pl.multiple_of((2 * pair + half) * BLK, BLK)
            kx = kix_ref[pl.ds(start, BLK), :]
            sc = None
            for h in range(N_HEADS):
                term = w[h:h + 1, :] * jnp.maximum(_dot(kx, rhs_scr[h]), 0.0)
                sc = term if sc is None else sc + term
            sc = jnp.where(start + srow <= tq, sc, NEG_INF)
            s_scr[pl.ds(start, BLK), :] = sc
            shi_scr[pl.ds(start, BLK), :] = sc.astype(BF16)
            colmax_scr[...] = jnp.maximum(colmax_scr[...], sc)
        return carry

    lax.fori_loop(0, (nblk + 1) // 2, fill, 0)

    def reduce_blocks(fn, init, combine):
        def body(grp, acc):
            for g in range(COUNT_GROUP):
                start = pl.multiple_of((grp * COUNT_GROUP + g) * BLK, BLK)
                blk = s_scr[pl.ds(start, BLK), :].reshape(BLK // 8, 8, BLK)
                acc = combine(acc, fn(blk, start))
            return acc
        return lax.fori_loop(0, ngroups, body, init)

    def fold4(m):
        return m.reshape(8, 4, 8, BLK).sum(axis=0)

    def total(acc):
        t = acc.sum(axis=0).sum(axis=0, keepdims=True)
        return jnp.broadcast_to(t, (8, BLK))

    zeros4 = jnp.zeros((4, 8, BLK), F32)
    add = lambda a, b: a + b

    def count(pred):
        return total(reduce_blocks(lambda blk, st: fold4(jnp.where(pred(blk, st), 1.0, 0.0)), zeros4, add))

    k_f = jnp.float32(k_top)

    one_b, zero_b = jnp.ones((), BF16), jnp.zeros((), BF16)

    def count_block_b(blk, thr_b):
        m = jnp.where(blk.reshape(BLK // 16, 16, BLK) >= thr_b[None], one_b, zero_b)
        parts = [m[r] for r in range(BLK // 16)]
        while len(parts) > 1:
            parts = [parts[n] + parts[n + 1] for n in range(0, len(parts), 2)]
        return parts[0].astype(F32)

    def spread(acc16):
        return jnp.broadcast_to(acc16.sum(axis=0, keepdims=True), (8, BLK))

    def count_upper(thr_b):
        def body(grp, acc):
            for g in range(COUNT_GROUP):
                start = pl.multiple_of((grp * COUNT_GROUP + g) * BLK, BLK)
                acc = acc + count_block_b(shi_scr[pl.ds(start, BLK), :], thr_b)
            return acc
        return spread(lax.fori_loop(0, ngroups, body, jnp.zeros((16, BLK), F32)))

    def grid_value_b(cand):
        return jnp.broadcast_to(_key16_to_float(cand)[0:1, :], (16, BLK)).astype(BF16)

    colmax_b = colmax_scr[...].astype(BF16)

    def ends_agree(carry):
        it, p, _ = carry
        cand = p | lax.shift_left(jnp.int32(1), 15 - it)
        cnt = spread(count_block_b(colmax_b, grid_value_b(cand)))
        all_above = cnt >= jnp.float32(BLK)
        settled = jnp.min(jnp.where(all_above | (cnt <= 0.0), 1.0, 0.0)) > 0.0
        return (jnp.where(settled, it + 1, it), jnp.where(settled & all_above, cand, p),
                settled.astype(jnp.int32))

    first_it, p0, _ = lax.while_loop(
        lambda c: jnp.logical_and(c[0] < 16, c[2] > 0), ends_agree,
        (jnp.int32(0), jnp.zeros((8, BLK), jnp.int32), jnp.int32(1)))

    def bisect_upper(it, p):
        cand = p | lax.shift_left(jnp.int32(1), 15 - it)
        return jnp.where(count_upper(grid_value_b(cand)) >= k_f, cand, p)

    p1 = lax.fori_loop(first_it, 16, bisect_upper, p0)
    edges = [_key16_to_float(p1 + d) for d in (-1, 0, 1)]
    counts4 = reduce_blocks(
        lambda blk, st: jnp.stack([fold4(jnp.where(blk >= e[None], 1.0, 0.0)) for e in edges]
                                  + [fold4(jnp.where(blk == 0.0, 1.0, 0.0))]),
        jnp.zeros((4, 4, 8, BLK), F32), add)
    c_lo, c_mid, c_hi, n_zero = (total(counts4[n]) for n in range(4))
    lower = c_mid < k_f
    p = jnp.where(lower, p1 - 1, p1)
    cnt_at_p = jnp.where(lower, c_lo, c_mid)
    cnt_above = jnp.where(lower, c_mid, c_hi)
    bracketed = (c_lo >= k_f) & (c_hi < k_f)
    tlane = i * BLK + lax.broadcasted_iota(jnp.int32, (8, BLK), 1)
    open_row = tlane + 1 < k_top
    bucket_lo = _key16_to_float(p)
    bucket_hi = _key16_to_float(p + 1)
    key_lo = jnp.where(p >= 0x8000, jnp.left_shift(p, 16), jnp.left_shift(p, 16) | 0xFFFF)
    bucket_size = cnt_at_p - cnt_above
    need_in = k_f - cnt_above

    zeros_in = jnp.where((bucket_lo <= 0.0) & (bucket_hi > 0.0), n_zero, 0.0)

    cell_scr[...] = jnp.full_like(cell_scr, NEG_INF)
    per_set = BLK // 8 // CELL_SETS

    def collect(grp, carry):
        xs = []
        for g in range(COUNT_GROUP):
            start = pl.multiple_of((grp * COUNT_GROUP + g) * BLK, BLK)
            blk = s_scr[pl.ds(start, BLK), :].reshape(per_set, CELL_SETS, 8, BLK)
            inside = (blk >= bucket_lo[None, None]) & (blk < bucket_hi[None, None]) & (blk != 0.0)
            xs.append(jnp.where(inside, blk, NEG_INF))
        for s in range(CELL_SETS):
            best = [cell_scr[d, s] for d in range(CELL_DEPTH)]
            for x in xs:
                for a in range(per_set):
                    v = x[a, s]
                    for d in range(CELL_DEPTH - 1):
                        best[d], v = jnp.maximum(best[d], v), jnp.minimum(best[d], v)
                    best[-1] = jnp.maximum(best[-1], v)
            for d in range(CELL_DEPTH):
                cell_scr[d, s] = best[d]
        return carry

    lax.fori_loop(0, ngroups, collect, 0)
    cells = cell_scr[...].reshape(CELL_DEPTH * CELL_SETS, 8, BLK)

    def count_cells(pred):
        t = jnp.where(pred, 1.0, 0.0).sum(axis=0).sum(axis=0, keepdims=True)
        return jnp.broadcast_to(t, (8, BLK))

    def count_bucket(rel):
        return lambda t: count_cells(rel(cells, t[None])) + jnp.where(rel(0.0, t), zeros_in, 0.0)

    ge, gt, eq = (lambda a, b: a >= b), (lambda a, b: a > b), (lambda a, b: a == b)
    kept = count_cells(cells > NEG_INF) + zeros_in
    redo = jnp.logical_not(open_row) & ((kept < bucket_size) | jnp.logical_not(bracketed))

    def bisect_cells(it, low):
        cand = low | lax.shift_left(jnp.int32(1), 15 - it)
        cnt = count_bucket(ge)(_key_to_float(key_lo + cand))
        return jnp.where(cnt >= need_in, cand, low)

    def key_float(key):
        return _key_to_float(jnp.where(open_row, jnp.int32(KEY_OF_NEG_INF), key))

    thr0 = key_float(key_lo + lax.fori_loop(0, 16, bisect_cells, jnp.zeros((8, BLK), jnp.int32)))
    thr_c = jnp.where(cells >= thr0[None], cells, jnp.inf).min(axis=0).min(axis=0, keepdims=True)
    thr_c = jnp.broadcast_to(thr_c, (8, BLK))
    thr_c = jnp.where((zeros_in > 0.0) & (thr0 <= 0.0), jnp.minimum(thr_c, 0.0), thr_c)
    thr_scr[...] = thr_c
    cgt_scr[...] = cnt_above + count_bucket(gt)(thr_c)
    ceq_scr[...] = count_bucket(eq)(thr_c)

    @pl.when(jnp.max(jnp.where(redo, 1.0, 0.0)) > 0.0)
    def _():
        def bisect_all(it, c):
            cand = c | lax.shift_left(jnp.int32(1), 31 - it)
            t = _key_to_float(cand)
            cnt = count(lambda blk, st: blk >= t[None])
            return jnp.where(cnt >= k_f, cand, c)

        t0 = key_float(lax.fori_loop(0, 32, bisect_all, jnp.zeros((8, BLK), jnp.int32)))
        mins = reduce_blocks(
            lambda blk, st: jnp.where(blk >= t0[None], blk, jnp.inf).reshape(8, 4, 8, BLK).min(axis=0),
            jnp.full((4, 8, BLK), jnp.inf, F32), jnp.minimum)
        t = jnp.broadcast_to(mins.min(axis=0).min(axis=0, keepdims=True), (8, BLK))
        both = reduce_blocks(
            lambda blk, st: jnp.stack([fold4(jnp.where(blk > t[None], 1.0, 0.0)),
                                       fold4(jnp.where(blk == t[None], 1.0, 0.0))]),
            jnp.zeros((2, 4, 8, BLK), F32), add)
        thr_scr[...] = t
        cgt_scr[...] = total(both[0])
        ceq_scr[...] = total(both[1])

    thr, c_gt, c_eq = thr_scr[...], cgt_scr[...], ceq_scr[...]
    need = k_f - c_gt
    finite_thr = thr > NEG_INF
    ambiguous = (c_eq > need) & finite_thr & (tlane < t_real)
    jst_scr[...] = jnp.where(finite_thr, jnp.int32(2 ** 30), jnp.int32(-1))

    @pl.when(jnp.max(jnp.where(ambiguous, 1.0, 0.0)) > 0.0)
    def _():
        sidx = lax.broadcasted_iota(jnp.int32, (BLK // 8, 8, BLK), 0) * 8 + \
            lax.broadcasted_iota(jnp.int32, (BLK // 8, 8, BLK), 1)

        def bisect_idx(it, jc):
            cand = jc | lax.shift_left(jnp.int32(1), 14 - it)
            cnt = count(lambda blk, st: (blk == thr[None]) & (st + sidx < cand[None]))
            return jnp.where(cnt < need, cand, jc)

        jc = lax.fori_loop(0, 15, bisect_idx, jnp.zeros((8, BLK), jnp.int32))
        jst_scr[...] = jnp.where(ambiguous, jc, jst_scr[...])

    thr_row = thr[0:1, :]
    jst_row = jst_scr[0:1, :]

    def emit(pair, carry):
        for kb in (2 * pair, 2 * pair + 1):
            start = pl.multiple_of(kb * BLK, BLK)
            blk = s_scr[pl.ds(start, BLK), :]
            sel = (blk > thr_row) | ((blk == thr_row) & (start + srow <= jst_row))
            mask_ref[0, kb] = jnp.where(sel, 0.0, NEG_INF).T.astype(BF16)
        return carry

    npairs = (nblk + 1) // 2
    lax.fori_loop(0, npairs, emit, 0)

    def blank(kb, carry):
        mask_ref[0, kb] = jnp.full((BLK, BLK), NEG_INF, BF16)
        return carry

    lax.fori_loop(2 * npairs, nkb, blank, 0)


def _select(kix, qixt, wixt, k_top, t_real):
    tp = kix.shape[0]
    nkb = tp // BLK
    assert tp < 2 ** 15 and nkb % 2 == 0 and k_top <= BLK
    padded = (nkb + COUNT_GROUP - 1) * BLK
    kern = functools.partial(_select_kernel, k_top=k_top, t_real=t_real)
    return pl.pallas_call(
        kern,
        grid=(nkb,),
        in_specs=[_resident(),
                  pl.BlockSpec((2, LANES, BLK), lambda i: (0, 0, i)),
                  pl.BlockSpec((LANES, BLK), lambda i: (0, i))],
        out_specs=pl.BlockSpec((1, nkb, BLK, BLK), lambda i: (i, 0, 0, 0)),
        out_shape=jax.ShapeDtypeStruct((nkb, nkb, BLK, BLK), BF16),
        scratch_shapes=[pltpu.VMEM((padded, BLK), F32),
                        pltpu.VMEM((padded, BLK), BF16),
                        pltpu.VMEM((BLK, BLK), F32),
                        pltpu.VMEM((N_HEADS, LANES, BLK), BF16),
                        pltpu.VMEM((CELL_DEPTH, CELL_SETS, 8, BLK), F32),
                        pltpu.VMEM((8, BLK), jnp.int32),
                        pltpu.VMEM((8, BLK), F32),
                        pltpu.VMEM((8, BLK), F32),
                        pltpu.VMEM((8, BLK), F32)],
        compiler_params=_params(),
        name="select_mask",
    )(kix, qixt, wixt)


def _dsa_kernel(q_ref, kt_ref, v_ref, mask_ref, o_ref, qm_scr, shift_scr, lsum_scr, acc_scr):
    i = pl.program_id(0)
    _split_pair_queries(q_ref, qm_scr)
    lane = lax.broadcasted_iota(jnp.int32, (BLK, LANES), 1)
    first_half = lane < D_HEAD

    def logits(h, kt, bias):
        return _dot(qm_scr[h], kt) + bias

    def accumulate(shifted):
        lsum_scr[...] = jnp.zeros_like(lsum_scr)
        acc_scr[...] = jnp.zeros_like(acc_scr)

        def body(pair, carry):
            for kb in (2 * pair, 2 * pair + 1):
                bias = mask_ref[0, kb].astype(F32)
                start = pl.multiple_of(kb * BLK, BLK)
                for j in range(N_PAIRS):
                    kt = kt_ref[j, kb]
                    vv = v_ref[j, pl.ds(start, BLK), :]
                    pvs = []
                    for h in (2 * j, 2 * j + 1):
                        s = logits(h, kt, bias)
                        if shifted:
                            m = shift_scr[h]
                            s = s - jnp.concatenate([m, m], axis=1)
                        p = jnp.exp(s)
                        lsum_scr[h] += p[:, :LANES] + p[:, LANES:]
                        pvs.append(_dot(p.astype(BF16), vv))
                    acc_scr[j] += jnp.where(first_half, pvs[0], pvs[1])
            return carry

        lax.fori_loop(0, (i + 2) // 2, body, 0)

    def row_sums():
        return [jnp.sum(lsum_scr[h], axis=1, keepdims=True) for h in range(N_HEADS)]

    accumulate(False)
    unsafe = jnp.zeros((BLK, 1), F32)
    for l in row_sums():
        unsafe = jnp.maximum(unsafe, jnp.where((l > SUM_SAFE_LO) & (l < SUM_SAFE_HI), 0.0, 1.0))
    acc_big = jnp.where(jnp.abs(acc_scr[...]) < SUM_SAFE_HI, 0.0, 1.0)

    @pl.when((jnp.max(unsafe) > 0.0) | (jnp.max(acc_big) > 0.0))
    def _():
        shift_scr[...] = jnp.full_like(shift_scr, NEG_INF)

        def body(kb, carry):
            bias = mask_ref[0, kb].astype(F32)
            for j in range(N_PAIRS):
                kt = kt_ref[j, kb]
                for h in (2 * j, 2 * j + 1):
                    s = logits(h, kt, bias)
                    shift_scr[h] = jnp.maximum(shift_scr[h], jnp.maximum(s[:, :LANES], s[:, LANES:]))
            return carry

        lax.fori_loop(0, i + 1, body, 0)
        for h in range(N_HEADS):
            m = jnp.max(shift_scr[h], axis=1, keepdims=True)
            m = jnp.where(m == NEG_INF, 0.0, m)
            shift_scr[h] = jnp.broadcast_to(m, (BLK, LANES))
        accumulate(True)

    sums = row_sums()
    for j in range(N_PAIRS):
        denom = jnp.where(first_half, sums[2 * j], sums[2 * j + 1])
        o_ref[:, j * LANES:(j + 1) * LANES] = (acc_scr[j] / denom).astype(BF16)


def _dsa_attention(q, kt, v, mask):
    tp = q.shape[1]
    nkb = tp // BLK
    return pl.pallas_call(
        _dsa_kernel,
        grid=(nkb,),
        in_specs=[pl.BlockSpec((N_PAIRS, BLK, LANES), lambda i: (0, i, 0)),
                  _resident(), _resident(),
                  pl.BlockSpec((1, nkb, BLK, BLK), lambda i: (i, 0, 0, 0))],
        out_specs=pl.BlockSpec((BLK, N_PAIRS * LANES), lambda i: (i, 0)),
        out_shape=jax.ShapeDtypeStruct((tp, N_PAIRS * LANES), BF16),
        scratch_shapes=[pltpu.VMEM((N_HEADS, BLK, LANES), BF16),
                        pltpu.VMEM((N_HEADS, BLK, LANES), F32),
                        pltpu.VMEM((N_HEADS, BLK, LANES), F32),
                        pltpu.VMEM((N_PAIRS, BLK, LANES), F32)],
        compiler_params=_params(),
        name="dsa_attention",
    )(q, kt, v, mask)


def _merge_kernel(osb_ref, ods_ref, gate_ref, h_ref, wsb_ref, wds_ref, wout_ref, g_ref, o_ref):
    gate = gate_ref[...].astype(F32)
    merged = (gate[:, :D_MODEL] * _dot(osb_ref[...], wsb_ref[...])
              + gate[:, D_MODEL:] * _dot(ods_ref[...], wds_ref[...]))
    mix = _dot(merged.astype(BF16), wout_ref[...])
    o_ref[...] = h_ref[...] + _rms(mix, g_ref[...])


def _merge(osb, ods, gates, h, wsb, wds, wout, g):
    tp = h.shape[0]
    tm = BLK
    full = lambda a: pl.BlockSpec(a.shape, lambda i: (0,) * a.ndim)
    rows = lambda n: pl.BlockSpec((tm, n), lambda i: (i, 0))
    return pl.pallas_call(
        _merge_kernel,
        grid=(tp // tm,),
        in_specs=[rows(N_PAIRS * LANES), rows(N_PAIRS * LANES), rows(2 * D_MODEL), rows(D_MODEL),
                  full(wsb), full(wds), full(wout), full(g)],
        out_specs=rows(D_MODEL),
        out_shape=jax.ShapeDtypeStruct((tp, D_MODEL), F32),
        compiler_params=_params(),
        name="merge_out",
    )(osb, ods, gates, h, wsb, wds, wout, g)


def _mlp_kernel(h_ref, gpre_ref, w1_ref, w2_ref, gpost_ref, o_ref):
    h = h_ref[...]
    hn = _rms(h, gpre_ref[...]).astype(BF16)
    ff = None
    for c in range(D_FF // D_MODEL):
        sl = slice(c * D_MODEL, (c + 1) * D_MODEL)
        u = jnp.maximum(_dot(hn, w1_ref[:, sl]), 0.0)
        part = _dot((u * u).astype(BF16), w2_ref[sl, :])
        ff = part if ff is None else ff + part
    o_ref[...] = h + _rms(ff, gpost_ref[...])


def _mlp(h, gpre, w1, w2, gpost):
    tp = h.shape[0]
    tm = BLK
    full = lambda a: pl.BlockSpec(a.shape, lambda i: (0,) * a.ndim)
    rows = pl.BlockSpec((tm, D_MODEL), lambda i: (i, 0))
    return pl.pallas_call(
        _mlp_kernel,
        grid=(tp // tm,),
        in_specs=[rows, full(gpre), full(w1), full(w2), full(gpost)],
        out_specs=rows,
        out_shape=jax.ShapeDtypeStruct((tp, D_MODEL), F32),
        compiler_params=_params(),
        name="mlp",
    )(h, gpre, w1, w2, gpost)


def _rotate_half_cols(w, d_head):
    rot = d_head // 4
    half = rot // 2
    n = w.shape[1] // d_head
    w3 = w.reshape(w.shape[0], n, d_head)
    out = jnp.zeros_like(w3)
    out = out.at[:, :, :half].set(-w3[:, :, half:rot])
    out = out.at[:, :, half:rot].set(w3[:, :, :half])
    return out.reshape(w.shape)


def _rope_tables(tp, d_head):
    rot = d_head // 4
    half = rot // 2
    inv_freq = jnp.power(jnp.float32(ROPE_THETA), -jnp.arange(half, dtype=F32) * (2.0 / rot))
    ang = jnp.arange(tp).astype(F32)[:, None] * inv_freq[None, :]
    cos, sin = jnp.cos(ang), jnp.sin(ang)
    ones = jnp.ones((tp, d_head - rot), F32)
    cos_h = jnp.concatenate([cos, cos, ones], axis=1)
    sin_h = jnp.concatenate([sin, sin, jnp.zeros_like(ones)], axis=1)
    reps = LANES // d_head
    return jnp.tile(cos_h, (1, reps)), jnp.tile(sin_h, (1, reps))


def kernel(x, meta_tokens, w_in, b_gate, w_branch_sb, w_branch_dsa, w_out, g_mix_pre, g_mix_post,
           w_mlp_in, w_mlp_out, g_mlp_pre, g_mlp_post):
    b, seq, d = x.shape
    assert b == 1 and d == D_MODEL and w_in.shape[0] == 1
    k_top = min(K_SEL_MAX, seq // 4)
    t_real = seq + N_META
    tp = -(-t_real // SEQ_ALIGN) * SEQ_ALIGN
    h = jnp.concatenate([meta_tokens.astype(x.dtype), x[0],
                         jnp.zeros((tp - t_real, d), x.dtype)], axis=0)

    w = w_in[0]
    hw = N_HEADS * D_HEAD
    iq = N_HEADS * D_IDX
    o = 0
    w_qsb, w_ksb, w_vsb = w[:, o:o + hw], w[:, o + hw:o + 2 * hw], w[:, o + 2 * hw:o + 3 * hw]
    o += 3 * hw
    w_qds, w_kds, w_vds = w[:, o:o + hw], w[:, o + hw:o + 2 * hw], w[:, o + 2 * hw:o + 3 * hw]
    o += 3 * hw
    w_qix, w_kix, w_wix = w[:, o:o + iq], w[:, o + iq:o + iq + D_IDX], w[:, o + iq + D_IDX:o + iq + D_IDX + N_HEADS]
    o += iq + D_IDX + N_HEADS
    w_gate = w[:, o:o + 2 * D_MODEL]
    w_qsb = w_qsb * (D_HEAD ** -0.5 * LOG2_E)
    w_qds = w_qds * (D_HEAD ** -0.5)
    w_wix = jnp.pad(w_wix * ((D_IDX * N_HEADS) ** -0.5), ((0, 0), (0, LANES - N_HEADS)))
    w_kix4 = jnp.tile(w_kix, (1, LANES // D_IDX))

    cos_ds, sin_ds = _rope_tables(tp, D_HEAD)
    cos_ix, sin_ix = _rope_tables(tp, D_IDX)
    cos = jnp.concatenate([cos_ds, cos_ix], axis=1)
    sin = jnp.concatenate([sin_ds, sin_ix], axis=1)

    bf = lambda a: a.astype(BF16)
    wp_r = bf(jnp.concatenate([w_qsb, w_vsb, w_vds], axis=1))
    wr_r = jnp.concatenate([w_qds, w_kix4], axis=1)
    wrot_r = jnp.concatenate([_rotate_half_cols(w_qds, D_HEAD), _rotate_half_cols(w_kix4, D_IDX)], axis=1)
    wp_c = bf(w_ksb.T)
    wr_c = jnp.concatenate([w_kds, w_qix], axis=1)
    wrot_c = jnp.concatenate([_rotate_half_cols(w_kds, D_HEAD), _rotate_half_cols(w_qix, D_IDX)], axis=1)

    hn = _prenorm(h, g_mix_pre)
    q_sb, v_sb, v_ds, q_ds, k_ix4, gates = _proj_rows(
        hn, wp_r, bf(wr_r), bf(wrot_r), cos, sin, bf(w_gate), b_gate)
    kt_sb, kt_ds, q_ixt, w_ixt = _proj_cols(
        hn, wp_c, bf(wr_c.T), bf(wrot_c.T), cos.T, sin.T, bf(w_wix.T))

    later_keys = (jnp.arange(BLK)[:, None] > jnp.arange(BLK)[None, :]).astype(BF16)
    o_sb = _sb_attention(q_sb, kt_sb, v_sb, later_keys)

    mask = _select(k_ix4, q_ixt, w_ixt, k_top, t_real)
    o_ds = _dsa_attention(q_ds, kt_ds, v_ds, mask)

    h1 = _merge(o_sb, o_ds, gates, h, bf(w_branch_sb[0]), bf(w_branch_dsa[0]), bf(w_out[0]), g_mix_post)
    h2 = _mlp(h1, g_mlp_pre, bf(w_mlp_in[0]), bf(w_mlp_out[0]), g_mlp_post)
    return h2[None, N_META:N_META + seq]
```

```python
import functools

import jax
import jax.numpy as jnp
from jax import lax
from jax.experimental import pallas as pl
from jax.experimental.pallas import tpu as pltpu

D_MODEL = 1024
D_HEAD = 64
N_HEADS = 8
N_PAIRS = N_HEADS // 2
D_IDX = 32
N_META = 16
K_SEL_MAX = 256
D_FF = 4 * D_MODEL
ROPE_THETA = 500000.0
RMS_EPS = 1e-6

LANES = 128
BLK = 256
SEQ_ALIGN = 512
VMEM_LIMIT = 60 * 1024 * 1024

EXP2_ZERO_BELOW = -151.0
LOG2_E = 1.4426950408889634
NEG_INF = float("-inf")
INT_MIN = -2 ** 31
KEY_OF_NEG_INF = 0x007FFFFF
CELL_SETS = 8
CELL_DEPTH = 4
COUNT_GROUP = 4
SUM_SAFE_LO = 1e-30
SUM_SAFE_HI = 1e30

F32 = jnp.float32
BF16 = jnp.bfloat16


def _dot(a, b):
    return jnp.dot(a, b, preferred_element_type=F32)


def _dot_nt(a, b):
    return lax.dot_general(a, b, (((1,), (1,)), ((), ())), preferred_element_type=F32)


def _rms(x, g):
    return x * lax.rsqrt(jnp.mean(x * x, axis=-1, keepdims=True) + RMS_EPS) * g


def _params(n_axes=1):
    return pltpu.CompilerParams(
        dimension_semantics=("arbitrary",) * n_axes, vmem_limit_bytes=VMEM_LIMIT)


def _resident():
    return pl.BlockSpec(memory_space=pltpu.VMEM)


def _prenorm_kernel(h_ref, g_ref, o_ref):
    o_ref[...] = _rms(h_ref[...], g_ref[...]).astype(BF16)


def _prenorm(h, g):
    tp = h.shape[0]
    tm = SEQ_ALIGN
    return pl.pallas_call(
        _prenorm_kernel,
        grid=(tp // tm,),
        in_specs=[pl.BlockSpec((tm, D_MODEL), lambda i: (i, 0)),
                  pl.BlockSpec((1, D_MODEL), lambda i: (0, 0))],
        out_specs=pl.BlockSpec((tm, D_MODEL), lambda i: (i, 0)),
        out_shape=jax.ShapeDtypeStruct((tp, D_MODEL), BF16),
        compiler_params=_params(),
        name="prenorm",
    )(h, g)


def _proj_rows_kernel(x_ref, wp_ref, wr_ref, wrot_ref, cos_ref, sin_ref, wg_ref, bg_ref,
                      qsb_ref, vsb_ref, vds_ref, qds_ref, kix_ref, gate_ref):
    x = x_ref[...]
    y = _dot(x, wp_ref[...])
    for n, ref in enumerate((qsb_ref, vsb_ref, vds_ref)):
        for g in range(N_PAIRS):
            c = (n * N_PAIRS + g) * LANES
            ref[g] = y[:, c:c + LANES].astype(BF16)
    a = _dot(x, wr_ref[...])
    b = _dot(x, wrot_ref[...])
    cos, sin = cos_ref[...], sin_ref[...]
    cds, sds = cos[:, :LANES], sin[:, :LANES]
    cix, six = cos[:, LANES:], sin[:, LANES:]

    def rope(g, c, s):
        sl = slice(g * LANES, (g + 1) * LANES)
        return (a[:, sl] * c + b[:, sl] * s).astype(BF16)

    for g in range(N_PAIRS):
        qds_ref[g] = rope(g, cds, sds)
    kix_ref[...] = rope(N_PAIRS, cix, six)
    gate_ref[...] = jax.nn.sigmoid(_dot(x, wg_ref[...]) + bg_ref[...]).astype(BF16)


def _proj_rows(hn, wp, wr, wrot, cos, sin, wg, bg):
    tp = hn.shape[0]
    tm = BLK
    full = lambda a: pl.BlockSpec(a.shape, lambda i: (0,) * a.ndim)
    grp = pl.BlockSpec((N_PAIRS, tm, LANES), lambda i: (0, i, 0))
    grp_shape = jax.ShapeDtypeStruct((N_PAIRS, tp, LANES), BF16)
    return pl.pallas_call(
        _proj_rows_kernel,
        grid=(tp // tm,),
        in_specs=[pl.BlockSpec((tm, D_MODEL), lambda i: (i, 0)), full(wp), full(wr), full(wrot),
                  pl.BlockSpec((tm, 2 * LANES), lambda i: (i, 0)),
                  pl.BlockSpec((tm, 2 * LANES), lambda i: (i, 0)),
                  full(wg), full(bg)],
        out_specs=[grp, grp, grp, grp,
                   pl.BlockSpec((tm, LANES), lambda i: (i, 0)),
                   pl.BlockSpec((tm, 2 * D_MODEL), lambda i: (i, 0))],
        out_shape=[grp_shape, grp_shape, grp_shape, grp_shape,
                   jax.ShapeDtypeStruct((tp, LANES), BF16),
                   jax.ShapeDtypeStruct((tp, 2 * D_MODEL), BF16)],
        compiler_params=_params(),
        name="proj_rows",
    )(hn, wp, wr, wrot, cos, sin, wg, bg)


def _proj_cols_kernel(x_ref, wp_ref, wr_ref, wrot_ref, cos_ref, sin_ref, ww_ref,
                      ksb_ref, kds_ref, qix_ref, wix_ref):
    x = x_ref[...]
    y = _dot_nt(wp_ref[...], x)
    for g in range(N_PAIRS):
        ksb_ref[g, 0] = y[g * LANES:(g + 1) * LANES, :].astype(BF16)
    a = _dot_nt(wr_ref[...], x)
    b = _dot_nt(wrot_ref[...], x)
    cos, sin = cos_ref[...], sin_ref[...]
    cds, sds = cos[:LANES, :], sin[:LANES, :]
    cix, six = cos[LANES:, :], sin[LANES:, :]

    def rope(g, c, s):
        sl = slice(g * LANES, (g + 1) * LANES)
        return (a[sl, :] * c + b[sl, :] * s).astype(BF16)

    for g in range(N_PAIRS):
        kds_ref[g, 0] = rope(g, cds, sds)
    for g in range(2):
        qix_ref[g] = rope(N_PAIRS + g, cix, six)
    wix_ref[...] = _dot_nt(ww_ref[...], x)


def _proj_cols(hn, wp, wr, wrot, cos, sin, ww):
    tp = hn.shape[0]
    nkb = tp // BLK
    full = lambda a: pl.BlockSpec(a.shape, lambda i: (0,) * a.ndim)
    kblk = pl.BlockSpec((N_PAIRS, 1, LANES, BLK), lambda i: (0, i, 0, 0))
    return pl.pallas_call(
        _proj_cols_kernel,
        grid=(nkb,),
        in_specs=[pl.BlockSpec((BLK, D_MODEL), lambda i: (i, 0)), full(wp), full(wr), full(wrot),
                  pl.BlockSpec((2 * LANES, BLK), lambda i: (0, i)),
                  pl.BlockSpec((2 * LANES, BLK), lambda i: (0, i)),
                  full(ww)],
        out_specs=[kblk, kblk,
                   pl.BlockSpec((2, LANES, BLK), lambda i: (0, 0, i)),
                   pl.BlockSpec((LANES, BLK), lambda i: (0, i))],
        out_shape=[jax.ShapeDtypeStruct((N_PAIRS, nkb, LANES, BLK), BF16),
                   jax.ShapeDtypeStruct((N_PAIRS, nkb, LANES, BLK), BF16),
                   jax.ShapeDtypeStruct((2, LANES, tp), BF16),
                   jax.ShapeDtypeStruct((LANES, tp), F32)],
        compiler_params=_params(),
        name="proj_cols",
    )(hn, wp, wr, wrot, cos, sin, ww)


def _split_pair_queries(q_ref, qm_scr):
    lane = lax.broadcasted_iota(jnp.int32, (BLK, LANES), 1)
    for j in range(N_PAIRS):
        qp = q_ref[j]
        zero = jnp.zeros_like(qp)
        qm_scr[2 * j] = jnp.where(lane < D_HEAD, qp, zero)
        qm_scr[2 * j + 1] = jnp.where(lane >= D_HEAD, qp, zero)


def _sb_kernel(q_ref, kt_ref, v_ref, u_ref, o_ref, qm_scr, later_scr, acc_scr):
    i = pl.program_id(0)
    _split_pair_queries(q_ref, qm_scr)
    acc_scr[...] = jnp.zeros_like(acc_scr)
    later_scr[...] = jnp.zeros_like(later_scr)
    row = lax.broadcasted_iota(jnp.int32, (BLK, BLK), 0)
    col = lax.broadcasted_iota(jnp.int32, (BLK, BLK), 1)
    visible_diag = col < row
    lane = lax.broadcasted_iota(jnp.int32, (BLK, LANES), 1)
    first_half = lane < D_HEAD
    u = u_ref[...]

    def block(kb, diag):
        start = pl.multiple_of(kb * BLK, BLK)
        worst = None
        for j in range(N_PAIRS):
            kt = kt_ref[j, kb]
            vv = v_ref[j, pl.ds(start, BLK), :]
            pvs = []
            for h in (2 * j, 2 * j + 1):
                z = _dot(qm_scr[h], kt)
                sp = jnp.maximum(z, 0.0) + jnp.log2(1.0 + jnp.exp2(-jnp.abs(z)))
                log_not = -sp
                if diag:
                    log_not = jnp.where(visible_diag, log_not, 0.0)
                inside = _dot(log_not.astype(BF16), u)
                later = later_scr[h]
                a = jnp.exp2(z - sp + inside + jnp.concatenate([later, later], axis=1))
                if diag:
                    a = jnp.where(visible_diag, a, 0.0)
                pvs.append(_dot(a.astype(BF16), vv))
                later = later + jnp.broadcast_to(inside[:, 0:1] + log_not[:, 0:1], (BLK, LANES))
                later_scr[h] = later
                worst = later if worst is None else jnp.maximum(worst, later)
            acc_scr[j] += jnp.where(first_half, pvs[0], pvs[1])
        return jnp.max(worst)

    def cond(carry):
        kb, worst = carry
        return jnp.logical_and(kb >= 0, worst > EXP2_ZERO_BELOW)

    def body(carry):
        kb, _ = carry
        return kb - 1, block(kb, False)

    lax.while_loop(cond, body, (i - 1, block(i, True)))

    for j in range(N_PAIRS):
        o_ref[:, j * LANES:(j + 1) * LANES] = acc_scr[j].astype(BF16)


def _sb_attention(q, kt, v, u):
    tp = q.shape[1]
    return pl.pallas_call(
        _sb_kernel,
        grid=(tp // BLK,),
        in_specs=[pl.BlockSpec((N_PAIRS, BLK, LANES), lambda i: (0, i, 0)),
                  _resident(), _resident(), _resident()],
        out_specs=pl.BlockSpec((BLK, N_PAIRS * LANES), lambda i: (i, 0)),
        out_shape=jax.ShapeDtypeStruct((tp, N_PAIRS * LANES), BF16),
        scratch_shapes=[pltpu.VMEM((N_HEADS, BLK, LANES), BF16),
                        pltpu.VMEM((N_HEADS, BLK, LANES), F32),
                        pltpu.VMEM((N_PAIRS, BLK, LANES), F32)],
        compiler_params=_params(),
        name="sb_attention",
    )(q, kt, v, u)


def _key_to_float(cand):
    neg_top = cand < 0
    bits = jnp.where(neg_top, cand ^ jnp.int32(INT_MIN), ~cand)
    f = lax.bitcast_convert_type(bits, F32)
    return jnp.where(jnp.logical_or(neg_top, cand > KEY_OF_NEG_INF), f, NEG_INF)


def _key16_to_float(p):
    top = p >= 0x8000
    pattern = jnp.where(top, p ^ 0x8000, p ^ 0xFFFF)
    f = lax.bitcast_convert_type(jnp.left_shift(pattern, 16), F32)
    return jnp.where(jnp.logical_or(top, p > (KEY_OF_NEG_INF >> 16)), f, NEG_INF)


def _select_kernel(kix_ref, qixt_ref, wixt_ref, mask_ref, s_scr, colmax_scr, rhs_scr, cell_scr,
                   jst_scr, thr_scr, cgt_scr, ceq_scr, *, k_top, t_real):
    i = pl.program_id(0)
    nblk = i + 1
    nkb = mask_ref.shape[1]
    sub = lax.broadcasted_iota(jnp.int32, (LANES, BLK), 0)
    for h in range(N_HEADS):
        src = qixt_ref[h // 4]
        rhs_scr[h] = jnp.where(sub // D_IDX == h % 4, src, jnp.zeros_like(src))
    w = wixt_ref[0:N_HEADS, :]
    srow = lax.broadcasted_iota(jnp.int32, (BLK, BLK), 0)
    tq = i * BLK + lax.broadcasted_iota(jnp.int32, (BLK, BLK), 1)

    for extra in range(COUNT_GROUP - 1):
        start = pl.multiple_of((nblk + extra) * BLK, BLK)
        s_scr[pl.ds(start, BLK), :] = jnp.full((BLK, BLK), NEG_INF, F32)
    colmax_scr[...] = jnp.full_like(colmax_scr, NEG_INF)
    ngroups = (nblk + COUNT_GROUP - 1) // COUNT_GROUP

    def fill(pair, carry):
        for half in range(2):
            start = pl.multiple_of((2 * pair + half) * BLK, BLK)
            kx = kix_ref[pl.ds(start, BLK), :]
            sc = None
            for h in range(N_HEADS):
                term = w[h:h + 1, :] * jnp.maximum(_dot(kx, rhs_scr[h]), 0.0)
                sc = term if sc is None else sc + term
            sc = jnp.where(start + srow <= tq, sc, NEG_INF)
            s_scr[pl.ds(start, BLK), :] = sc
            colmax_scr[...] = jnp.maximum(colmax_scr[...], sc)
        return carry

    lax.fori_loop(0, (nblk + 1) // 2, fill, 0)

    def reduce_blocks(fn, init, combine):
        def body(grp, acc):
            for g in range(COUNT_GROUP):
                start = pl.multiple_of((grp * COUNT_GROUP + g) * BLK, BLK)
                blk = s_scr[pl.ds(start, BLK), :].reshape(BLK // 8, 8, BLK)
                acc = combine(acc, fn(blk, start))
            return acc
        return lax.fori_loop(0, ngroups, body, init)

    def fold4(m):
        return m.reshape(8, 4, 8, BLK).sum(axis=0)

    def total(acc):
        t = acc.sum(axis=0).sum(axis=0, keepdims=True)
        return jnp.broadcast_to(t, (8, BLK))

    zeros4 = jnp.zeros((4, 8, BLK), F32)
    add = lambda a, b: a + b

    def count(pred):
        return total(reduce_blocks(lambda blk, st: fold4(jnp.where(pred(blk, st), 1.0, 0.0)), zeros4, add))

    k_f = jnp.float32(k_top)

    colmax = colmax_scr[...].reshape(BLK // 8, 8, BLK)

    def colmax_ends(it, ends):
        p_min, p_max = ends
        bit = lax.shift_left(jnp.int32(1), 15 - it)
        n_min = total(fold4(jnp.where(colmax >= _key16_to_float(p_min | bit)[None], 1.0, 0.0)))
        n_max = total(fold4(jnp.where(colmax >= _key16_to_float(p_max | bit)[None], 1.0, 0.0)))
        return (jnp.where(n_min >= jnp.float32(BLK), p_min | bit, p_min),
                jnp.where(n_max >= 1.0, p_max | bit, p_max))

    zero_p = jnp.zeros((8, BLK), jnp.int32)
    p_min, p_max = lax.fori_loop(0, 16, colmax_ends, (zero_p, zero_p))

    def narrow(carry):
        lo, hi, _ = carry
        mid = jnp.right_shift(lo + hi, 1)
        t = _key16_to_float(mid)
        ok = count(lambda blk, st: blk >= t[None]) >= k_f
        lo, hi = jnp.where(ok, mid, lo), jnp.where(ok, hi, mid)
        return lo, hi, jnp.max(hi - lo)

    lo0, hi0 = p_min, p_max + 1
    p, _, _ = lax.while_loop(lambda c: c[2] > 1, narrow, (lo0, hi0, jnp.max(hi0 - lo0)))
    tlane = i * BLK + lax.broadcasted_iota(jnp.int32, (8, BLK), 1)
    open_row = tlane + 1 < k_top
    bucket_lo = _key16_to_float(p)
    bucket_hi = _key16_to_float(p + 1)
    key_lo = jnp.where(p >= 0x8000, jnp.left_shift(p, 16), jnp.left_shift(p, 16) | 0xFFFF)

    cell_scr[...] = jnp.full_like(cell_scr, NEG_INF)
    per_set = BLK // 8 // CELL_SETS

    def collect(grp, tallies):
        xs = []
        for g in range(COUNT_GROUP):
            start = pl.multiple_of((grp * COUNT_GROUP + g) * BLK, BLK)
            blk = s_scr[pl.ds(start, BLK), :].reshape(per_set, CELL_SETS, 8, BLK)
            at_least = blk >= bucket_lo[None, None]
            above = blk >= bucket_hi[None, None]
            zero = blk == 0.0
            inside = at_least & jnp.logical_not(above) & jnp.logical_not(zero)
            xs.append(jnp.where(inside, blk, NEG_INF))
            tallies = tallies + jnp.stack(
                [jnp.where(m, 1.0, 0.0).sum(axis=0) for m in (at_least, above, zero)])
        for s in range(CELL_SETS):
            best = [cell_scr[d, s] for d in range(CELL_DEPTH)]
            for x in xs:
                for a in range(per_set):
                    v = x[a, s]
                    for d in range(CELL_DEPTH - 1):
                        best[d], v = jnp.maximum(best[d], v), jnp.minimum(best[d], v)
                    best[-1] = jnp.maximum(best[-1], v)
            for d in range(CELL_DEPTH):
                cell_scr[d, s] = best[d]
        return tallies

    tallies = lax.fori_loop(0, ngroups, collect, jnp.zeros((3, CELL_SETS, 8, BLK), F32))
    cnt_at_p, cnt_above, n_zero = (total(tallies[n]) for n in range(3))
    bracketed = (cnt_at_p >= k_f) & (cnt_above < k_f)
    bucket_size = cnt_at_p - cnt_above
    need_in = k_f - cnt_above
    zeros_in = jnp.where((bucket_lo <= 0.0) & (bucket_hi > 0.0), n_zero, 0.0)
    cells = cell_scr[...].reshape(CELL_DEPTH * CELL_SETS, 8, BLK)

    def count_cells(pred):
        t = jnp.where(pred, 1.0, 0.0).sum(axis=0).sum(axis=0, keepdims=True)
        return jnp.broadcast_to(t, (8, BLK))

    def count_bucket(rel):
        return lambda t: count_cells(rel(cells, t[None])) + jnp.where(rel(0.0, t), zeros_in, 0.0)

    ge, gt, eq = (lambda a, b: a >= b), (lambda a, b: a > b), (lambda a, b: a == b)
    kept = count_cells(cells > NEG_INF) + zeros_in
    redo = jnp.logical_not(open_row) & ((kept < bucket_size) | jnp.logical_not(bracketed))

    def bisect_cells(it, low):
        cand = low | lax.shift_left(jnp.int32(1), 15 - it)
        cnt = count_bucket(ge)(_key_to_float(key_lo + cand))
        return jnp.where(cnt >= need_in, cand, low)

    def key_float(key):
        return _key_to_float(jnp.where(open_row, jnp.int32(KEY_OF_NEG_INF), key))

    thr0 = key_float(key_lo + lax.fori_loop(0, 16, bisect_cells, jnp.zeros((8, BLK), jnp.int32)))
    thr_c = jnp.where(cells >= thr0[None], cells, jnp.inf).min(axis=0).min(axis=0, keepdims=True)
    thr_c = jnp.broadcast_to(thr_c, (8, BLK))
    thr_c = jnp.where((zeros_in > 0.0) & (thr0 <= 0.0), jnp.minimum(thr_c, 0.0), thr_c)
    thr_scr[...] = thr_c
    cgt_scr[...] = cnt_above + count_bucket(gt)(thr_c)
    ceq_scr[...] = count_bucket(eq)(thr_c)

    @pl.when(jnp.max(jnp.where(redo, 1.0, 0.0)) > 0.0)
    def _():
        def bisect_all(it, c):
            cand = c | lax.shift_left(jnp.int32(1), 31 - it)
            t = _key_to_float(cand)
            cnt = count(lambda blk, st: blk >= t[None])
            return jnp.where(cnt >= k_f, cand, c)

        t0 = key_float(lax.fori_loop(0, 32, bisect_all, jnp.zeros((8, BLK), jnp.int32)))
        mins = reduce_blocks(
            lambda blk, st: jnp.where(blk >= t0[None], blk, jnp.inf).reshape(8, 4, 8, BLK).min(axis=0),
            jnp.full((4, 8, BLK), jnp.inf, F32), jnp.minimum)
        t = jnp.broadcast_to(mins.min(axis=0).min(axis=0, keepdims=True), (8, BLK))
        both = reduce_blocks(
            lambda blk, st: jnp.stack([fold4(jnp.where(blk > t[None], 1.0, 0.0)),
                                       fold4(jnp.where(blk == t[None], 1.0, 0.0))]),
            jnp.zeros((2, 4, 8, BLK), F32), add)
        thr_scr[...] = t
        cgt_scr[...] = total(both[0])
        ceq_scr[...] = total(both[1])

    thr, c_gt, c_eq = thr_scr[...], cgt_scr[...], ceq_scr[...]
    need = k_f - c_gt
    finite_thr = thr > NEG_INF
    ambiguous = (c_eq > need) & finite_thr & (tlane < t_real)
    jst_scr[...] = jnp.where(finite_thr, jnp.int32(2 ** 30), jnp.int32(-1))

    @pl.when(jnp.max(jnp.where(ambiguous, 1.0, 0.0)) > 0.0)
    def _():
        sidx = lax.broadcasted_iota(jnp.int32, (BLK // 8, 8, BLK), 0) * 8 + \
            lax.broadcasted_iota(jnp.int32, (BLK // 8, 8, BLK), 1)

        def bisect_idx(it, jc):
            cand = jc | lax.shift_left(jnp.int32(1), 14 - it)
            cnt = count(lambda blk, st: (blk == thr[None]) & (st + sidx < cand[None]))
            return jnp.where(cnt < need, cand, jc)

        jc = lax.fori_loop(0, 15, bisect_idx, jnp.zeros((8, BLK), jnp.int32))
        jst_scr[...] = jnp.where(ambiguous, jc, jst_scr[...])

    thr_row = thr[0:1, :]
    jst_row = jst_scr[0:1, :]

    def emit(pair, carry):
        for kb in (2 * pair, 2 * pair + 1):
            start = pl.multiple_of(kb * BLK, BLK)
            blk = s_scr[pl.ds(start, BLK), :]
            sel = (blk > thr_row) | ((blk == thr_row) & (start + srow <= jst_row))
            mask_ref[0, kb] = jnp.where(sel, 0.0, NEG_INF).T.astype(BF16)
        return carry

    npairs = (nblk + 1) // 2
    lax.fori_loop(0, npairs, emit, 0)

    def blank(kb, carry):
        mask_ref[0, kb] = jnp.full((BLK, BLK), NEG_INF, BF16)
        return carry

    lax.fori_loop(2 * npairs, nkb, blank, 0)


def _select(kix, qixt, wixt, k_top, t_real):
    tp = kix.shape[0]
    nkb = tp // BLK
    assert tp < 2 ** 15 and nkb % 2 == 0 and k_top <= BLK
    padded = (nkb + COUNT_GROUP - 1) * BLK
    kern = functools.partial(_select_kernel, k_top=k_top, t_real=t_real)
    return pl.pallas_call(
        kern,
        grid=(nkb,),
        in_specs=[_resident(),
                  pl.BlockSpec((2, LANES, BLK), lambda i: (0, 0, i)),
                  pl.BlockSpec((LANES, BLK), lambda i: (0, i))],
        out_specs=pl.BlockSpec((1, nkb, BLK, BLK), lambda i: (i, 0, 0, 0)),
        out_shape=jax.ShapeDtypeStruct((nkb, nkb, BLK, BLK), BF16),
        scratch_shapes=[pltpu.VMEM((padded, BLK), F32),
                        pltpu.VMEM((BLK, BLK), F32),
                        pltpu.VMEM((N_HEADS, LANES, BLK), BF16),
                        pltpu.VMEM((CELL_DEPTH, CELL_SETS, 8, BLK), F32),
                        pltpu.VMEM((8, BLK), jnp.int32),
                        pltpu.VMEM((8, BLK), F32),
                        pltpu.VMEM((8, BLK), F32),
                        pltpu.VMEM((8, BLK), F32)],
        compiler_params=_params(),
        name="select_mask",
    )(kix, qixt, wixt)


def _dsa_kernel(q_ref, kt_ref, v_ref, mask_ref, o_ref, qm_scr, shift_scr, lsum_scr, acc_scr):
    i = pl.program_id(0)
    _split_pair_queries(q_ref, qm_scr)
    lane = lax.broadcasted_iota(jnp.int32, (BLK, LANES), 1)
    first_half = lane < D_HEAD

    def logits(h, kt, bias):
        return _dot(qm_scr[h], kt) + bias

    def accumulate(shifted):
        lsum_scr[...] = jnp.zeros_like(lsum_scr)
        acc_scr[...] = jnp.zeros_like(acc_scr)

        def body(pair, carry):
            for kb in (2 * pair, 2 * pair + 1):
                bias = mask_ref[0, kb].astype(F32)
                start = pl.multiple_of(kb * BLK, BLK)
                for j in range(N_PAIRS):
                    kt = kt_ref[j, kb]
                    vv = v_ref[j, pl.ds(start, BLK), :]
                    pvs = []
                    for h in (2 * j, 2 * j + 1):
                        s = logits(h, kt, bias)
                        if shifted:
                            m = shift_scr[h]
                            s = s - jnp.concatenate([m, m], axis=1)
                        p = jnp.exp(s)
                        lsum_scr[h] += p[:, :LANES] + p[:, LANES:]
                        pvs.append(_dot(p.astype(BF16), vv))
                    acc_scr[j] += jnp.where(first_half, pvs[0], pvs[1])
            return carry

        lax.fori_loop(0, (i + 2) // 2, body, 0)

    def row_sums():
        return [jnp.sum(lsum_scr[h], axis=1, keepdims=True) for h in range(N_HEADS)]

    accumulate(False)
    unsafe = jnp.zeros((BLK, 1), F32)
    for l in row_sums():
        unsafe = jnp.maximum(unsafe, jnp.where((l > SUM_SAFE_LO) & (l < SUM_SAFE_HI), 0.0, 1.0))
    acc_big = jnp.where(jnp.abs(acc_scr[...]) < SUM_SAFE_HI, 0.0, 1.0)

    @pl.when((jnp.max(unsafe) > 0.0) | (jnp.max(acc_big) > 0.0))
    def _():
        shift_scr[...] = jnp.full_like(shift_scr, NEG_INF)

        def body(kb, carry):
            bias = mask_ref[0, kb].astype(F32)
            for j in range(N_PAIRS):
                kt = kt_ref[j, kb]
                for h in (2 * j, 2 * j + 1):
                    s = logits(h, kt, bias)
                    shift_scr[h] = jnp.maximum(shift_scr[h], jnp.maximum(s[:, :LANES], s[:, LANES:]))
            return carry

        lax.fori_loop(0, i + 1, body, 0)
        for h in range(N_HEADS):
            m = jnp.max(shift_scr[h], axis=1, keepdims=True)
            m = jnp.where(m == NEG_INF, 0.0, m)
            shift_scr[h] = jnp.broadcast_to(m, (BLK, LANES))
        accumulate(True)

    sums = row_sums()
    for j in range(N_PAIRS):
        denom = jnp.where(first_half, sums[2 * j], sums[2 * j + 1])
        o_ref[:, j * LANES:(j + 1) * LANES] = (acc_scr[j] / denom).astype(BF16)


def _dsa_attention(q, kt, v, mask):
    tp = q.shape[1]
    nkb = tp // BLK
    return pl.pallas_call(
        _dsa_kernel,
        grid=(nkb,),
        in_specs=[pl.BlockSpec((N_PAIRS, BLK, LANES), lambda i: (0, i, 0)),
                  _resident(), _resident(),
                  pl.BlockSpec((1, nkb, BLK, BLK), lambda i: (i, 0, 0, 0))],
        out_specs=pl.BlockSpec((BLK, N_PAIRS * LANES), lambda i: (i, 0)),
        out_shape=jax.ShapeDtypeStruct((tp, N_PAIRS * LANES), BF16),
        scratch_shapes=[pltpu.VMEM((N_HEADS, BLK, LANES), BF16),
                        pltpu.VMEM((N_HEADS, BLK, LANES), F32),
                        pltpu.VMEM((N_HEADS, BLK, LANES), F32),
                        pltpu.VMEM((N_PAIRS, BLK, LANES), F32)],
        compiler_params=_params(),
        name="dsa_attention",
    )(q, kt, v, mask)


def _merge_kernel(osb_ref, ods_ref, gate_ref, h_ref, wsb_ref, wds_ref, wout_ref, g_ref, o_ref):
    gate = gate_ref[...].astype(F32)
    merged = (gate[:, :D_MODEL] * _dot(osb_ref[...], wsb_ref[...])
              + gate[:, D_MODEL:] * _dot(ods_ref[...], wds_ref[...]))
    mix = _dot(merged.astype(BF16), wout_ref[...])
    o_ref[...] = h_ref[...] + _rms(mix, g_ref[...])


def _merge(osb, ods, gates, h, wsb, wds, wout, g):
    tp = h.shape[0]
    tm = BLK
    full = lambda a: pl.BlockSpec(a.shape, lambda i: (0,) * a.ndim)
    rows = lambda n: pl.BlockSpec((tm, n), lambda i: (i, 0))
    return pl.pallas_call(
        _merge_kernel,
        grid=(tp // tm,),
        in_specs=[rows(N_PAIRS * LANES), rows(N_PAIRS * LANES), rows(2 * D_MODEL), rows(D_MODEL),
                  full(wsb), full(wds), full(wout), full(g)],
        out_specs=rows(D_MODEL),
        out_shape=jax.ShapeDtypeStruct((tp, D_MODEL), F32),
        compiler_params=_params(),
        name="merge_out",
    )(osb, ods, gates, h, wsb, wds, wout, g)


def _mlp_kernel(h_ref, gpre_ref, w1_ref, w2_ref, gpost_ref, o_ref):
    h = h_ref[...]
    hn = _rms(h, gpre_ref[...]).astype(BF16)
    ff = None
    for c in range(D_FF // D_MODEL):
        sl = slice(c * D_MODEL, (c + 1) * D_MODEL)
        u = jnp.maximum(_dot(hn, w1_ref[:, sl]), 0.0)
        part = _dot((u * u).astype(BF16), w2_ref[sl, :])
        ff = part if ff is None else ff + part
    o_ref[...] = h + _rms(ff, gpost_ref[...])


def _mlp(h, gpre, w1, w2, gpost):
    tp = h.shape[0]
    tm = BLK
    full = lambda a: pl.BlockSpec(a.shape, lambda i: (0,) * a.ndim)
    rows = pl.BlockSpec((tm, D_MODEL), lambda i: (i, 0))
    return pl.pallas_call(
        _mlp_kernel,
        grid=(tp // tm,),
        in_specs=[rows, full(gpre), full(w1), full(w2), full(gpost)],
        out_specs=rows,
        out_shape=jax.ShapeDtypeStruct((tp, D_MODEL), F32),
        compiler_params=_params(),
        name="mlp",
    )(h, gpre, w1, w2, gpost)


def _rotate_half_cols(w, d_head):
    rot = d_head // 4
    half = rot // 2
    n = w.shape[1] // d_head
    w3 = w.reshape(w.shape[0], n, d_head)
    out = jnp.zeros_like(w3)
    out = out.at[:, :, :half].set(-w3[:, :, half:rot])
    out = out.at[:, :, half:rot].set(w3[:, :, :half])
    return out.reshape(w.shape)


def _rope_tables(tp, d_head):
    rot = d_head // 4
    half = rot // 2
    inv_freq = jnp.power(jnp.float32(ROPE_THETA), -jnp.arange(half, dtype=F32) * (2.0 / rot))
    ang = jnp.arange(tp).astype(F32)[:, None] * inv_freq[None, :]
    cos, sin = jnp.cos(ang), jnp.sin(ang)
    ones = jnp.ones((tp, d_head - rot), F32)
    cos_h = jnp.concatenate([cos, cos, ones], axis=1)
    sin_h = jnp.concatenate([sin, sin, jnp.zeros_like(ones)], axis=1)
    reps = LANES // d_head
    return jnp.tile(cos_h, (1, reps)), jnp.tile(sin_h, (1, reps))


def kernel(x, meta_tokens, w_in, b_gate, w_branch_sb, w_branch_dsa, w_out, g_mix_pre, g_mix_post,
           w_mlp_in, w_mlp_out, g_mlp_pre, g_mlp_post):
    b, seq, d = x.shape
    assert b == 1 and d == D_MODEL and w_in.shape[0] == 1
    k_top = min(K_SEL_MAX, seq // 4)
    t_real = seq + N_META
    tp = -(-t_real // SEQ_ALIGN) * SEQ_ALIGN
    h = jnp.concatenate([meta_tokens.astype(x.dtype), x[0],
                         jnp.zeros((tp - t_real, d), x.dtype)], axis=0)

    w = w_in[0]
    hw = N_HEADS * D_HEAD
    iq = N_HEADS * D_IDX
    o = 0
    w_qsb, w_ksb, w_vsb = w[:, o:o + hw], w[:, o + hw:o + 2 * hw], w[:, o + 2 * hw:o + 3 * hw]
    o += 3 * hw
    w_qds, w_kds, w_vds = w[:, o:o + hw], w[:, o + hw:o + 2 * hw], w[:, o + 2 * hw:o + 3 * hw]
    o += 3 * hw
    w_qix, w_kix, w_wix = w[:, o:o + iq], w[:, o + iq:o + iq + D_IDX], w[:, o + iq + D_IDX:o + iq + D_IDX + N_HEADS]
    o += iq + D_IDX + N_HEADS
    w_gate = w[:, o:o + 2 * D_MODEL]
    w_qsb = w_qsb * (D_HEAD ** -0.5 * LOG2_E)
    w_qds = w_qds * (D_HEAD ** -0.5)
    w_wix = jnp.pad(w_wix * ((D_IDX * N_HEADS) ** -0.5), ((0, 0), (0, LANES - N_HEADS)))
    w_kix4 = jnp.tile(w_kix, (1, LANES // D_IDX))

    cos_ds, sin_ds = _rope_tables(tp, D_HEAD)
    cos_ix, sin_ix = _rope_tables(tp, D_IDX)
    cos = jnp.concatenate([cos_ds, cos_ix], axis=1)
    sin = jnp.concatenate([sin_ds, sin_ix], axis=1)

    bf = lambda a: a.astype(BF16)
    wp_r = bf(jnp.concatenate([w_qsb, w_vsb, w_vds], axis=1))
    wr_r = jnp.concatenate([w_qds, w_kix4], axis=1)
    wrot_r = jnp.concatenate([_rotate_half_cols(w_qds, D_HEAD), _rotate_half_cols(w_kix4, D_IDX)], axis=1)
    wp_c = bf(w_ksb.T)
    wr_c = jnp.concatenate([w_kds, w_qix], axis=1)
    wrot_c = jnp.concatenate([_rotate_half_cols(w_kds, D_HEAD), _rotate_half_cols(w_qix, D_IDX)], axis=1)

    hn = _prenorm(h, g_mix_pre)
    q_sb, v_sb, v_ds, q_ds, k_ix4, gates = _proj_rows(
        hn, wp_r, bf(wr_r), bf(wrot_r), cos, sin, bf(w_gate), b_gate)
    kt_sb, kt_ds, q_ixt, w_ixt = _proj_cols(
        hn, wp_c, bf(wr_c.T), bf(wrot_c.T), cos.T, sin.T, bf(w_wix.T))

    later_keys = (jnp.arange(BLK)[:, None] > jnp.arange(BLK)[None, :]).astype(BF16)
    o_sb = _sb_attention(q_sb, kt_sb, v_sb, later_keys)

    mask = _select(k_ix4, q_ixt, w_ixt, k_top, t_real)
    o_ds = _dsa_attention(q_ds, kt_ds, v_ds, mask)

    h1 = _merge(o_sb, o_ds, gates, h, bf(w_branch_sb[0]), bf(w_branch_dsa[0]), bf(w_out[0]), g_mix_post)
    h2 = _mlp(h1, g_mlp_pre, bf(w_mlp_in[0]), bf(w_mlp_out[0]), g_mlp_post)
    return h2[None, N_META:N_META + seq]
```

```python
import functools

import jax
import jax.numpy as jnp
from jax import lax
from jax.experimental import pallas as pl
from jax.experimental.pallas import tpu as pltpu

D_MODEL = 1024
D_HEAD = 64
N_HEADS = 8
N_PAIRS = N_HEADS // 2
D_IDX = 32
N_META = 16
K_SEL_MAX = 256
D_FF = 4 * D_MODEL
ROPE_THETA = 500000.0
RMS_EPS = 1e-6

LANES = 128
BLK = 256
SEQ_ALIGN = 512
VMEM_LIMIT = 60 * 1024 * 1024

EXP2_ZERO_BELOW = -151.0
LOG2_E = 1.4426950408889634
NEG_INF = float("-inf")
INT_MIN = -2 ** 31
KEY_OF_NEG_INF = 0x007FFFFF
CELL_SETS = 8
CELL_DEPTH = 4
COUNT_GROUP = 4
SUM_SAFE_LO = 1e-30
SUM_SAFE_HI = 1e30

F32 = jnp.float32
BF16 = jnp.bfloat16


def _dot(a, b):
    return jnp.dot(a, b, preferred_element_type=F32)


def _dot_nt(a, b):
    return lax.dot_general(a, b, (((1,), (1,)), ((), ())), preferred_element_type=F32)


def _rms(x, g):
    return x * lax.rsqrt(jnp.mean(x * x, axis=-1, keepdims=True) + RMS_EPS) * g


def _params(n_axes=1):
    return pltpu.CompilerParams(
        dimension_semantics=("arbitrary",) * n_axes, vmem_limit_bytes=VMEM_LIMIT)


def _resident():
    return pl.BlockSpec(memory_space=pltpu.VMEM)


def _prenorm_kernel(h_ref, g_ref, o_ref):
    o_ref[...] = _rms(h_ref[...], g_ref[...]).astype(BF16)


def _prenorm(h, g):
    tp = h.shape[0]
    tm = SEQ_ALIGN
    return pl.pallas_call(
        _prenorm_kernel,
        grid=(tp // tm,),
        in_specs=[pl.BlockSpec((tm, D_MODEL), lambda i: (i, 0)),
                  pl.BlockSpec((1, D_MODEL), lambda i: (0, 0))],
        out_specs=pl.BlockSpec((tm, D_MODEL), lambda i: (i, 0)),
        out_shape=jax.ShapeDtypeStruct((tp, D_MODEL), BF16),
        compiler_params=_params(),
        name="prenorm",
    )(h, g)


def _proj_rows_kernel(x_ref, wp_ref, wr_ref, wrot_ref, cos_ref, sin_ref, wg_ref, bg_ref,
                      qsb_ref, vsb_ref, vds_ref, qds_ref, kix_ref, gate_ref):
    x = x_ref[...]
    y = _dot(x, wp_ref[...])
    for n, ref in enumerate((qsb_ref, vsb_ref, vds_ref)):
        for g in range(N_PAIRS):
            c = (n * N_PAIRS + g) * LANES
            ref[g] = y[:, c:c + LANES].astype(BF16)
    a = _dot(x, wr_ref[...])
    b = _dot(x, wrot_ref[...])
    cos, sin = cos_ref[...], sin_ref[...]
    cds, sds = cos[:, :LANES], sin[:, :LANES]
    cix, six = cos[:, LANES:], sin[:, LANES:]

    def rope(g, c, s):
        sl = slice(g * LANES, (g + 1) * LANES)
        return (a[:, sl] * c + b[:, sl] * s).astype(BF16)

    for g in range(N_PAIRS):
        qds_ref[g] = rope(g, cds, sds)
    kix_ref[...] = rope(N_PAIRS, cix, six)
    gate_ref[...] = jax.nn.sigmoid(_dot(x, wg_ref[...]) + bg_ref[...]).astype(BF16)


def _proj_rows(hn, wp, wr, wrot, cos, sin, wg, bg):
    tp = hn.shape[0]
    tm = BLK
    full = lambda a: pl.BlockSpec(a.shape, lambda i: (0,) * a.ndim)
    grp = pl.BlockSpec((N_PAIRS, tm, LANES), lambda i: (0, i, 0))
    grp_shape = jax.ShapeDtypeStruct((N_PAIRS, tp, LANES), BF16)
    return pl.pallas_call(
        _proj_rows_kernel,
        grid=(tp // tm,),
        in_specs=[pl.BlockSpec((tm, D_MODEL), lambda i: (i, 0)), full(wp), full(wr), full(wrot),
                  pl.BlockSpec((tm, 2 * LANES), lambda i: (i, 0)),
                  pl.BlockSpec((tm, 2 * LANES), lambda i: (i, 0)),
                  full(wg), full(bg)],
        out_specs=[grp, grp, grp, grp,
                   pl.BlockSpec((tm, LANES), lambda i: (i, 0)),
                   pl.BlockSpec((tm, 2 * D_MODEL), lambda i: (i, 0))],
        out_shape=[grp_shape, grp_shape, grp_shape, grp_shape,
                   jax.ShapeDtypeStruct((tp, LANES), BF16),
                   jax.ShapeDtypeStruct((tp, 2 * D_MODEL), BF16)],
        compiler_params=_params(),
        name="proj_rows",
    )(hn, wp, wr, wrot, cos, sin, wg, bg)


def _proj_cols_kernel(x_ref, wp_ref, wr_ref, wrot_ref, cos_ref, sin_ref, ww_ref,
                      ksb_ref, kds_ref, qix_ref, wix_ref):
    x = x_ref[...]
    y = _dot_nt(wp_ref[...], x)
    for g in range(N_PAIRS):
        ksb_ref[g, 0] = y[g * LANES:(g + 1) * LANES, :].astype(BF16)
    a = _dot_nt(wr_ref[...], x)
    b = _dot_nt(wrot_ref[...], x)
    cos, sin = cos_ref[...], sin_ref[...]
    cds, sds = cos[:LANES, :], sin[:LANES, :]
    cix, six = cos[LANES:, :], sin[LANES:, :]

    def rope(g, c, s):
        sl = slice(g * LANES, (g + 1) * LANES)
        return (a[sl, :] * c + b[sl, :] * s).astype(BF16)

    for g in range(N_PAIRS):
        kds_ref[g, 0] = rope(g, cds, sds)
    for g in range(2):
        qix_ref[g] = rope(N_PAIRS + g, cix, six)
    wix_ref[...] = _dot_nt(ww_ref[...], x)


def _proj_cols(hn, wp, wr, wrot, cos, sin, ww):
    tp = hn.shape[0]
    nkb = tp // BLK
    full = lambda a: pl.BlockSpec(a.shape, lambda i: (0,) * a.ndim)
    kblk = pl.BlockSpec((N_PAIRS, 1, LANES, BLK), lambda i: (0, i, 0, 0))
    return pl.pallas_call(
        _proj_cols_kernel,
        grid=(nkb,),
        in_specs=[pl.BlockSpec((BLK, D_MODEL), lambda i: (i, 0)), full(wp), full(wr), full(wrot),
                  pl.BlockSpec((2 * LANES, BLK), lambda i: (0, i)),
                  pl.BlockSpec((2 * LANES, BLK), lambda i: (0, i)),
                  full(ww)],
        out_specs=[kblk, kblk,
                   pl.BlockSpec((2, LANES, BLK), lambda i: (0, 0, i)),
                   pl.BlockSpec((LANES, BLK), lambda i: (0, i))],
        out_shape=[jax.ShapeDtypeStruct((N_PAIRS, nkb, LANES, BLK), BF16),
                   jax.ShapeDtypeStruct((N_PAIRS, nkb, LANES, BLK), BF16),
                   jax.ShapeDtypeStruct((2, LANES, tp), BF16),
                   jax.ShapeDtypeStruct((LANES, tp), F32)],
        compiler_params=_params(),
        name="proj_cols",
    )(hn, wp, wr, wrot, cos, sin, ww)


def _split_pair_queries(q_ref, qm_scr):
    lane = lax.broadcasted_iota(jnp.int32, (BLK, LANES), 1)
    for j in range(N_PAIRS):
        qp = q_ref[j]
        zero = jnp.zeros_like(qp)
        qm_scr[2 * j] = jnp.where(lane < D_HEAD, qp, zero)
        qm_scr[2 * j + 1] = jnp.where(lane >= D_HEAD, qp, zero)


def _sb_kernel(q_ref, kt_ref, v_ref, u_ref, o_ref, qm_scr, later_scr, acc_scr):
    i = pl.program_id(0)
    _split_pair_queries(q_ref, qm_scr)
    acc_scr[...] = jnp.zeros_like(acc_scr)
    later_scr[...] = jnp.zeros_like(later_scr)
    row = lax.broadcasted_iota(jnp.int32, (BLK, BLK), 0)
    col = lax.broadcasted_iota(jnp.int32, (BLK, BLK), 1)
    visible_diag = col < row
    lane = lax.broadcasted_iota(jnp.int32, (BLK, LANES), 1)
    first_half = lane < D_HEAD
    u = u_ref[...]

    def block(kb, diag):
        start = pl.multiple_of(kb * BLK, BLK)
        worst = None
        for j in range(N_PAIRS):
            kt = kt_ref[j, kb]
            vv = v_ref[j, pl.ds(start, BLK), :]
            pvs = []
            for h in (2 * j, 2 * j + 1):
                z = _dot(qm_scr[h], kt)
                sp = jnp.maximum(z, 0.0) + jnp.log2(1.0 + jnp.exp2(-jnp.abs(z)))
                log_not = -sp
                if diag:
                    log_not = jnp.where(visible_diag, log_not, 0.0)
                inside = _dot(log_not.astype(BF16), u)
                later = later_scr[h]
                a = jnp.exp2(z - sp + inside + jnp.concatenate([later, later], axis=1))
                if diag:
                    a = jnp.where(visible_diag, a, 0.0)
                pvs.append(_dot(a.astype(BF16), vv))
                later = later + jnp.broadcast_to(inside[:, 0:1] + log_not[:, 0:1], (BLK, LANES))
                later_scr[h] = later
                worst = later if worst is None else jnp.maximum(worst, later)
            acc_scr[j] += jnp.where(first_half, pvs[0], pvs[1])
        return jnp.max(worst)

    def cond(carry):
        kb, worst = carry
        return jnp.logical_and(kb >= 0, worst > EXP2_ZERO_BELOW)

    def body(carry):
        kb, _ = carry
        return kb - 1, block(kb, False)

    lax.while_loop(cond, body, (i - 1, block(i, True)))

    for j in range(N_PAIRS):
        o_ref[:, j * LANES:(j + 1) * LANES] = acc_scr[j].astype(BF16)


def _sb_attention(q, kt, v, u):
    tp = q.shape[1]
    return pl.pallas_call(
        _sb_kernel,
        grid=(tp // BLK,),
        in_specs=[pl.BlockSpec((N_PAIRS, BLK, LANES), lambda i: (0, i, 0)),
                  _resident(), _resident(), _resident()],
        out_specs=pl.BlockSpec((BLK, N_PAIRS * LANES), lambda i: (i, 0)),
        out_shape=jax.ShapeDtypeStruct((tp, N_PAIRS * LANES), BF16),
        scratch_shapes=[pltpu.VMEM((N_HEADS, BLK, LANES), BF16),
                        pltpu.VMEM((N_HEADS, BLK, LANES), F32),
                        pltpu.VMEM((N_PAIRS, BLK, LANES), F32)],
        compiler_params=_params(),
        name="sb_attention",
    )(q, kt, v, u)


def _key_to_float(cand):
    neg_top = cand < 0
    bits = jnp.where(neg_top, cand ^ jnp.int32(INT_MIN), ~cand)
    f = lax.bitcast_convert_type(bits, F32)
    return jnp.where(jnp.logical_or(neg_top, cand > KEY_OF_NEG_INF), f, NEG_INF)


def _key16_to_float(p):
    top = p >= 0x8000
    pattern = jnp.where(top, p ^ 0x8000, p ^ 0xFFFF)
    f = lax.bitcast_convert_type(jnp.left_shift(pattern, 16), F32)
    return jnp.where(jnp.logical_or(top, p > (KEY_OF_NEG_INF >> 16)), f, NEG_INF)


def _select_kernel(kix_ref, qixt_ref, wixt_ref, mask_ref, s_scr, shi_scr, colmax_scr, rhs_scr, cell_scr,
                   jst_scr, thr_scr, cgt_scr, ceq_scr, *, k_top, t_real):
    i = pl.program_id(0)
    nblk = i + 1
    nkb = mask_ref.shape[1]
    sub = lax.broadcasted_iota(jnp.int32, (LANES, BLK), 0)
    for h in range(N_HEADS):
        src = qixt_ref[h // 4]
        rhs_scr[h] = jnp.where(sub // D_IDX == h % 4, src, jnp.zeros_like(src))
    w = wixt_ref[0:N_HEADS, :]
    srow = lax.broadcasted_iota(jnp.int32, (BLK, BLK), 0)
    tq = i * BLK + lax.broadcasted_iota(jnp.int32, (BLK, BLK), 1)

    for extra in range(COUNT_GROUP - 1):
        start = pl.multiple_of((nblk + extra) * BLK, BLK)
        s_scr[pl.ds(start, BLK), :] = jnp.full((BLK, BLK), NEG_INF, F32)
        shi_scr[pl.ds(start, BLK), :] = jnp.full((BLK, BLK), NEG_INF, BF16)
    colmax_scr[...] = jnp.full_like(colmax_scr, NEG_INF)
    ngroups = (nblk + COUNT_GROUP - 1) // COUNT_GROUP

    def fill(pair, carry):
        for half in range(2):
            start = pl.multiple_of((2 * pair + half) * BLK, BLK)
            kx = kix_ref[pl.ds(start, BLK), :]
            sc = None
            for h in range(N_HEADS):
                term = w[h:h + 1, :] * jnp.maximum(_dot(kx, rhs_scr[h]), 0.0)
                sc = term if sc is None else sc + term
            sc = jnp.where(start + srow <= tq, sc, NEG_INF)
            s_scr[pl.ds(start, BLK), :] = sc
            shi_scr[pl.ds(start, BLK), :] = sc.astype(BF16)
            colmax_scr[...] = jnp.maximum(colmax_scr[...], sc)
        return carry

    lax.fori_loop(0, (nblk + 1) // 2, fill, 0)

    def reduce_blocks(fn, init, combine):
        def body(grp, acc):
            for g in range(COUNT_GROUP):
                start = pl.multiple_of((grp * COUNT_GROUP + g) * BLK, BLK)
                blk = s_scr[pl.ds(start, BLK), :].reshape(BLK // 8, 8, BLK)
                acc = combine(acc, fn(blk, start))
            return acc
        return lax.fori_loop(0, ngroups, body, init)

    def fold4(m):
        return m.reshape(8, 4, 8, BLK).sum(axis=0)

    def total(acc):
        t = acc.sum(axis=0).sum(axis=0, keepdims=True)
        return jnp.broadcast_to(t, (8, BLK))

    zeros4 = jnp.zeros((4, 8, BLK), F32)
    add = lambda a, b: a + b

    def count(pred):
        return total(reduce_blocks(lambda blk, st: fold4(jnp.where(pred(blk, st), 1.0, 0.0)), zeros4, add))

    k_f = jnp.float32(k_top)

    one_b, zero_b = jnp.ones((), BF16), jnp.zeros((), BF16)

    def count_block_b(blk, thr_b):
        m = jnp.where(blk.reshape(BLK // 16, 16, BLK) >= thr_b[None], one_b, zero_b)
        parts = [m[r] for r in range(BLK // 16)]
        while len(parts) > 1:
            parts = [parts[n] + parts[n + 1] for n in range(0, len(parts), 2)]
        return parts[0].astype(F32)

    def spread(acc16):
        return jnp.broadcast_to(acc16.sum(axis=0, keepdims=True), (8, BLK))

    def count_upper(thr_b):
        def body(grp, acc):
            for g in range(COUNT_GROUP):
                start = pl.multiple_of((grp * COUNT_GROUP + g) * BLK, BLK)
                acc = acc + count_block_b(shi_scr[pl.ds(start, BLK), :], thr_b)
            return acc
        return spread(lax.fori_loop(0, ngroups, body, jnp.zeros((16, BLK), F32)))

    def grid_value_b(cand):
        return jnp.broadcast_to(_key16_to_float(cand)[0:1, :], (16, BLK)).astype(BF16)

    colmax_b = colmax_scr[...].astype(BF16)

    def ends_agree(carry):
        it, p, _ = carry
        cand = p | lax.shift_left(jnp.int32(1), 15 - it)
        cnt = spread(count_block_b(colmax_b, grid_value_b(cand)))
        all_above = cnt >= jnp.float32(BLK)
        settled = jnp.min(jnp.where(all_above | (cnt <= 0.0), 1.0, 0.0)) > 0.0
        return (jnp.where(settled, it + 1, it), jnp.where(settled & all_above, cand, p),
                settled.astype(jnp.int32))

    first_it, p0, _ = lax.while_loop(
        lambda c: jnp.logical_and(c[0] < 16, c[2] > 0), ends_agree,
        (jnp.int32(0), jnp.zeros((8, BLK), jnp.int32), jnp.int32(1)))

    def bisect_upper(it, p):
        cand = p | lax.shift_left(jnp.int32(1), 15 - it)
        return jnp.where(count_upper(grid_value_b(cand)) >= k_f, cand, p)

    p1 = lax.fori_loop(first_it, 16, bisect_upper, p0)
    edges = [_key16_to_float(p1 + d) for d in (-1, 0, 1)]
    counts4 = reduce_blocks(
        lambda blk, st: jnp.stack([fold4(jnp.where(blk >= e[None], 1.0, 0.0)) for e in edges]
                                  + [fold4(jnp.where(blk == 0.0, 1.0, 0.0))]),
        jnp.zeros((4, 4, 8, BLK), F32), add)
    c_lo, c_mid, c_hi, n_zero = (total(counts4[n]) for n in range(4))
    lower = c_mid < k_f
    p = jnp.where(lower, p1 - 1, p1)
    cnt_at_p = jnp.where(lower, c_lo, c_mid)
    cnt_above = jnp.where(lower, c_mid, c_hi)
    bracketed = (c_lo >= k_f) & (c_hi < k_f)
    tlane = i * BLK + lax.broadcasted_iota(jnp.int32, (8, BLK), 1)
    open_row = tlane + 1 < k_top
    bucket_lo = _key16_to_float(p)
    bucket_hi = _key16_to_float(p + 1)
    key_lo = jnp.where(p >= 0x8000, jnp.left_shift(p, 16), jnp.left_shift(p, 16) | 0xFFFF)
    bucket_size = cnt_at_p - cnt_above
    need_in = k_f - cnt_above

    zeros_in = jnp.where((bucket_lo <= 0.0) & (bucket_hi > 0.0), n_zero, 0.0)

    cell_scr[...] = jnp.full_like(cell_scr, NEG_INF)
    per_set = BLK // 8 // CELL_SETS

    def collect(grp, carry):
        xs = []
        for g in range(COUNT_GROUP):
            start = pl.multiple_of((grp * COUNT_GROUP + g) * BLK, BLK)
            blk = s_scr[pl.ds(start, BLK), :].reshape(per_set, CELL_SETS, 8, BLK)
            inside = (blk >= bucket_lo[None, None]) & (blk < bucket_hi[None, None]) & (blk != 0.0)
            xs.append(jnp.where(inside, blk, NEG_INF))
        for s in range(CELL_SETS):
            best = [cell_scr[d, s] for d in range(CELL_DEPTH)]
            for x in xs:
                for a in range(per_set):
                    v = x[a, s]
                    for d in range(CELL_DEPTH - 1):
                        best[d], v = jnp.maximum(best[d], v), jnp.minimum(best[d], v)
                    best[-1] = jnp.maximum(best[-1], v)
            for d in range(CELL_DEPTH):
                cell_scr[d, s] = best[d]
        return carry

    lax.fori_loop(0, ngroups, collect, 0)
    cells = cell_scr[...].reshape(CELL_DEPTH * CELL_SETS, 8, BLK)

    def count_cells(pred):
        t = jnp.where(pred, 1.0, 0.0).sum(axis=0).sum(axis=0, keepdims=True)
        return jnp.broadcast_to(t, (8, BLK))

    def count_bucket(rel):
        return lambda t: count_cells(rel(cells, t[None])) + jnp.where(rel(0.0, t), zeros_in, 0.0)

    ge, gt, eq = (lambda a, b: a >= b), (lambda a, b: a > b), (lambda a, b: a == b)
    kept = count_cells(cells > NEG_INF) + zeros_in
    redo = jnp.logical_not(open_row) & ((kept < bucket_size) | jnp.logical_not(bracketed))

    def bisect_cells(it, low):
        cand = low | lax.shift_left(jnp.int32(1), 15 - it)
        cnt = count_bucket(ge)(_key_to_float(key_lo + cand))
        return jnp.where(cnt >= need_in, cand, low)

    def key_float(key):
        return _key_to_float(jnp.where(open_row, jnp.int32(KEY_OF_NEG_INF), key))

    thr0 = key_float(key_lo + lax.fori_loop(0, 16, bisect_cells, jnp.zeros((8, BLK), jnp.int32)))
    thr_c = jnp.where(cells >= thr0[None], cells, jnp.inf).min(axis=0).min(axis=0, keepdims=True)
    thr_c = jnp.broadcast_to(thr_c, (8, BLK))
    thr_c = jnp.where((zeros_in > 0.0) & (thr0 <= 0.0), jnp.minimum(thr_c, 0.0), thr_c)
    thr_scr[...] = thr_c
    cgt_scr[...] = cnt_above + count_bucket(gt)(thr_c)
    ceq_scr[...] = count_bucket(eq)(thr_c)

    @pl.when(jnp.max(jnp.where(redo, 1.0, 0.0)) > 0.0)
    def _():
        def bisect_all(it, c):
            cand = c | lax.shift_left(jnp.int32(1), 31 - it)
            t = _key_to_float(cand)
            cnt = count(lambda blk, st: blk >= t[None])
            return jnp.where(cnt >= k_f, cand, c)

        t0 = key_float(lax.fori_loop(0, 32, bisect_all, jnp.zeros((8, BLK), jnp.int32)))
        mins = reduce_blocks(
            lambda blk, st: jnp.where(blk >= t0[None], blk, jnp.inf).reshape(8, 4, 8, BLK).min(axis=0),
            jnp.full((4, 8, BLK), jnp.inf, F32), jnp.minimum)
        t = jnp.broadcast_to(mins.min(axis=0).min(axis=0, keepdims=True), (8, BLK))
        both = reduce_blocks(
            lambda blk, st: jnp.stack([fold4(jnp.where(blk > t[None], 1.0, 0.0)),
                                       fold4(jnp.where(blk == t[None], 1.0, 0.0))]),
            jnp.zeros((2, 4, 8, BLK), F32), add)
        thr_scr[...] = t
        cgt_scr[...] = total(both[0])
        ceq_scr[...] = total(both[1])

    thr, c_gt, c_eq = thr_scr[...], cgt_scr[...], ceq_scr[...]
    need = k_f - c_gt
    finite_thr = thr > NEG_INF
    ambiguous = (c_eq > need) & finite_thr & (tlane < t_real)
    jst_scr[...] = jnp.where(finite_thr, jnp.int32(2 ** 30), jnp.int32(-1))

    @pl.when(jnp.max(jnp.where(ambiguous, 1.0, 0.0)) > 0.0)
    def _():
        sidx = lax.broadcasted_iota(jnp.int32, (BLK // 8, 8, BLK), 0) * 8 + \
            lax.broadcasted_iota(jnp.int32, (BLK // 8, 8, BLK), 1)

        def bisect_idx(it, jc):
            cand = jc | lax.shift_left(jnp.int32(1), 14 - it)
            cnt = count(lambda blk, st: (blk == thr[None]) & (st + sidx < cand[None]))
            return jnp.where(cnt < need, cand, jc)

        jc = lax.fori_loop(0, 15, bisect_idx, jnp.zeros((8, BLK), jnp.int32))
        jst_scr[...] = jnp.where(ambiguous, jc, jst_scr[...])

    thr_row = thr[0:1, :]
    jst_row = jst_scr[0:1, :]

    def emit(pair, carry):
        for kb in (2 * pair, 2 * pair + 1):
            start = pl.multiple_of(kb * BLK, BLK)
            blk = s_scr[pl.ds(start, BLK), :]
            sel = (blk > thr_row) | ((blk == thr_row) & (start + srow <= jst_row))
            mask_ref[0, kb] = jnp.where(sel, 0.0, NEG_INF).T.astype(BF16)
        return carry

    npairs = (nblk + 1) // 2
    lax.fori_loop(0, npairs, emit, 0)

    def blank(kb, carry):
        mask_ref[0, kb] = jnp.full((BLK, BLK), NEG_INF, BF16)
        return carry

    lax.fori_loop(2 * npairs, nkb, blank, 0)


def _select(kix, qixt, wixt, k_top, t_real):
    tp = kix.shape[0]
    nkb = tp // BLK
    assert tp < 2 ** 15 and nkb % 2 == 0 and k_top <= BLK
    padded = (nkb + COUNT_GROUP - 1) * BLK
    kern = functools.partial(_select_kernel, k_top=k_top, t_real=t_real)
    return pl.pallas_call(
        kern,
        grid=(nkb,),
        in_specs=[_resident(),
                  pl.BlockSpec((2, LANES, BLK), lambda i: (0, 0, i)),
                  pl.BlockSpec((LANES, BLK), lambda i: (0, i))],
        out_specs=pl.BlockSpec((1, nkb, BLK, BLK), lambda i: (i, 0, 0, 0)),
        out_shape=jax.ShapeDtypeStruct((nkb, nkb, BLK, BLK), BF16),
        scratch_shapes=[pltpu.VMEM((padded, BLK), F32),
                        pltpu.VMEM((padded, BLK), BF16),
                        pltpu.VMEM((BLK, BLK), F32),
                        pltpu.VMEM((N_HEADS, LANES, BLK), BF16),
                        pltpu.VMEM((CELL_DEPTH, CELL_SETS, 8, BLK), F32),
                        pltpu.VMEM((8, BLK), jnp.int32),
                        pltpu.VMEM((8, BLK), F32),
                        pltpu.VMEM((8, BLK), F32),
                        pltpu.VMEM((8, BLK), F32)],
        compiler_params=_params(),
        name="select_mask",
    )(kix, qixt, wixt)


def _dsa_kernel(q_ref, kt_ref, v_ref, mask_ref, o_ref, qm_scr, shift_scr, lsum_scr, acc_scr):
    i = pl.program_id(0)
    _split_pair_queries(q_ref, qm_scr)
    lane = lax.broadcasted_iota(jnp.int32, (BLK, LANES), 1)
    first_half = lane < D_HEAD

    def logits(h, kt, bias):
        return _dot(qm_scr[h], kt) + bias

    def accumulate(shifted):
        lsum_scr[...] = jnp.zeros_like(lsum_scr)
        acc_scr[...] = jnp.zeros_like(acc_scr)

        def blocks(first, count):
            for kb in range(count):
                kb = first + kb
                bias = mask_ref[0, kb].astype(F32)
                start = pl.multiple_of(kb * BLK, BLK)
                for j in range(N_PAIRS):
                    kt = kt_ref[j, kb]
                    vv = v_ref[j, pl.ds(start, BLK), :]
                    pvs = []
                    for h in (2 * j, 2 * j + 1):
                        s = logits(h, kt, bias)
                        if shifted:
                            m = shift_scr[h]
                            s = s - jnp.concatenate([m, m], axis=1)
                        p = jnp.exp(s)
                        lsum_scr[h] += p[:, :LANES] + p[:, LANES:]
                        pvs.append(_dot(p.astype(BF16), vv))
                    acc_scr[j] += jnp.where(first_half, pvs[0], pvs[1])

        npairs = (i + 2) // 2
        nquads = npairs // 2

        def body(quad, carry):
            blocks(4 * quad, 4)
            return carry

        lax.fori_loop(0, nquads, body, 0)

        @pl.when(npairs % 2 == 1)
        def _():
            blocks(4 * nquads, 2)

    def row_sums():
        return [jnp.sum(lsum_scr[h], axis=1, keepdims=True) for h in range(N_HEADS)]

    accumulate(False)
    unsafe = jnp.zeros((BLK, 1), F32)
    for l in row_sums():
        unsafe = jnp.maximum(unsafe, jnp.where((l > SUM_SAFE_LO) & (l < SUM_SAFE_HI), 0.0, 1.0))
    acc_big = jnp.where(jnp.abs(acc_scr[...]) < SUM_SAFE_HI, 0.0, 1.0)

    @pl.when((jnp.max(unsafe) > 0.0) | (jnp.max(acc_big) > 0.0))
    def _():
        shift_scr[...] = jnp.full_like(shift_scr, NEG_INF)

        def body(kb, carry):
            bias = mask_ref[0, kb].astype(F32)
            for j in range(N_PAIRS):
                kt = kt_ref[j, kb]
                for h in (2 * j, 2 * j + 1):
                    s = logits(h, kt, bias)
                    shift_scr[h] = jnp.maximum(shift_scr[h], jnp.maximum(s[:, :LANES], s[:, LANES:]))
            return carry

        lax.fori_loop(0, i + 1, body, 0)
        for h in range(N_HEADS):
            m = jnp.max(shift_scr[h], axis=1, keepdims=True)
            m = jnp.where(m == NEG_INF, 0.0, m)
            shift_scr[h] = jnp.broadcast_to(m, (BLK, LANES))
        accumulate(True)

    sums = row_sums()
    for j in range(N_PAIRS):
        denom = jnp.where(first_half, sums[2 * j], sums[2 * j + 1])
        o_ref[:, j * LANES:(j + 1) * LANES] = (acc_scr[j] / denom).astype(BF16)


def _dsa_attention(q, kt, v, mask):
    tp = q.shape[1]
    nkb = tp // BLK
    return pl.pallas_call(
        _dsa_kernel,
        grid=(nkb,),
        in_specs=[pl.BlockSpec((N_PAIRS, BLK, LANES), lambda i: (0, i, 0)),
                  _resident(), _resident(),
                  pl.BlockSpec((1, nkb, BLK, BLK), lambda i: (i, 0, 0, 0))],
        out_specs=pl.BlockSpec((BLK, N_PAIRS * LANES), lambda i: (i, 0)),
        out_shape=jax.ShapeDtypeStruct((tp, N_PAIRS * LANES), BF16),
        scratch_shapes=[pltpu.VMEM((N_HEADS, BLK, LANES), BF16),
                        pltpu.VMEM((N_HEADS, BLK, LANES), F32),
                        pltpu.VMEM((N_HEADS, BLK, LANES), F32),
                        pltpu.VMEM((N_PAIRS, BLK, LANES), F32)],
        compiler_params=_params(),
        name="dsa_attention",
    )(q, kt, v, mask)


def _merge_kernel(osb_ref, ods_ref, gate_ref, h_ref, wsb_ref, wds_ref, wout_ref, g_ref, o_ref):
    gate = gate_ref[...].astype(F32)
    merged = (gate[:, :D_MODEL] * _dot(osb_ref[...], wsb_ref[...])
              + gate[:, D_MODEL:] * _dot(ods_ref[...], wds_ref[...]))
    mix = _dot(merged.astype(BF16), wout_ref[...])
    o_ref[...] = h_ref[...] + _rms(mix, g_ref[...])


def _merge(osb, ods, gates, h, wsb, wds, wout, g):
    tp = h.shape[0]
    tm = BLK
    full = lambda a: pl.BlockSpec(a.shape, lambda i: (0,) * a.ndim)
    rows = lambda n: pl.BlockSpec((tm, n), lambda i: (i, 0))
    return pl.pallas_call(
        _merge_kernel,
        grid=(tp // tm,),
        in_specs=[rows(N_PAIRS * LANES), rows(N_PAIRS * LANES), rows(2 * D_MODEL), rows(D_MODEL),
                  full(wsb), full(wds), full(wout), full(g)],
        out_specs=rows(D_MODEL),
        out_shape=jax.ShapeDtypeStruct((tp, D_MODEL), F32),
        compiler_params=_params(),
        name="merge_out",
    )(osb, ods, gates, h, wsb, wds, wout, g)


def _mlp_kernel(h_ref, gpre_ref, w1_ref, w2_ref, gpost_ref, o_ref):
    h = h_ref[...]
    hn = _rms(h, gpre_ref[...]).astype(BF16)
    ff = None
    for c in range(D_FF // D_MODEL):
        sl = slice(c * D_MODEL, (c + 1) * D_MODEL)
        u = jnp.maximum(_dot(hn, w1_ref[:, sl]), 0.0)
        part = _dot((u * u).astype(BF16), w2_ref[sl, :])
        ff = part if ff is None else ff + part
    o_ref[...] = h + _rms(ff, gpost_ref[...])


def _mlp(h, gpre, w1, w2, gpost):
    tp = h.shape[0]
    tm = BLK
    full = lambda a: pl.BlockSpec(a.shape, lambda i: (0,) * a.ndim)
    rows = pl.BlockSpec((tm, D_MODEL), lambda i: (i, 0))
    return pl.pallas_call(
        _mlp_kernel,
        grid=(tp // tm,),
        in_specs=[rows, full(gpre), full(w1), full(w2), full(gpost)],
        out_specs=rows,
        out_shape=jax.ShapeDtypeStruct((tp, D_MODEL), F32),
        compiler_params=_params(),
        name="mlp",
    )(h, gpre, w1, w2, gpost)


def _rotate_half_cols(w, d_head):
    rot = d_head // 4
    half = rot // 2
    n = w.shape[1] // d_head
    w3 = w.reshape(w.shape[0], n, d_head)
    out = jnp.zeros_like(w3)
    out = out.at[:, :, :half].set(-w3[:, :, half:rot])
    out = out.at[:, :, half:rot].set(w3[:, :, :half])
    return out.reshape(w.shape)


def _rope_tables(tp, d_head):
    rot = d_head // 4
    half = rot // 2
    inv_freq = jnp.power(jnp.float32(ROPE_THETA), -jnp.arange(half, dtype=F32) * (2.0 / rot))
    ang = jnp.arange(tp).astype(F32)[:, None] * inv_freq[None, :]
    cos, sin = jnp.cos(ang), jnp.sin(ang)
    ones = jnp.ones((tp, d_head - rot), F32)
    cos_h = jnp.concatenate([cos, cos, ones], axis=1)
    sin_h = jnp.concatenate([sin, sin, jnp.zeros_like(ones)], axis=1)
    reps = LANES // d_head
    return jnp.tile(cos_h, (1, reps)), jnp.tile(sin_h, (1, reps))


def kernel(x, meta_tokens, w_in, b_gate, w_branch_sb, w_branch_dsa, w_out, g_mix_pre, g_mix_post,
           w_mlp_in, w_mlp_out, g_mlp_pre, g_mlp_post):
    b, seq, d = x.shape
    assert b == 1 and d == D_MODEL and w_in.shape[0] == 1
    k_top = min(K_SEL_MAX, seq // 4)
    t_real = seq + N_META
    tp = -(-t_real // SEQ_ALIGN) * SEQ_ALIGN
    h = jnp.concatenate([meta_tokens.astype(x.dtype), x[0],
                         jnp.zeros((tp - t_real, d), x.dtype)], axis=0)

    w = w_in[0]
    hw = N_HEADS * D_HEAD
    iq = N_HEADS * D_IDX
    o = 0
    w_qsb, w_ksb, w_vsb = w[:, o:o + hw], w[:, o + hw:o + 2 * hw], w[:, o + 2 * hw:o + 3 * hw]
    o += 3 * hw
    w_qds, w_kds, w_vds = w[:, o:o + hw], w[:, o + hw:o + 2 * hw], w[:, o + 2 * hw:o + 3 * hw]
    o += 3 * hw
    w_qix, w_kix, w_wix = w[:, o:o + iq], w[:, o + iq:o + iq + D_IDX], w[:, o + iq + D_IDX:o + iq + D_IDX + N_HEADS]
    o += iq + D_IDX + N_HEADS
    w_gate = w[:, o:o + 2 * D_MODEL]
    w_qsb = w_qsb * (D_HEAD ** -0.5 * LOG2_E)
    w_qds = w_qds * (D_HEAD ** -0.5)
    w_wix = jnp.pad(w_wix * ((D_IDX * N_HEADS) ** -0.5), ((0, 0), (0, LANES - N_HEADS)))
    w_kix4 = jnp.tile(w_kix, (1, LANES // D_IDX))

    cos_ds, sin_ds = _rope_tables(tp, D_HEAD)
    cos_ix, sin_ix = _rope_tables(tp, D_IDX)
    cos = jnp.concatenate([cos_ds, cos_ix], axis=1)
    sin = jnp.concatenate([sin_ds, sin_ix], axis=1)

    bf = lambda a: a.astype(BF16)
    wp_r = bf(jnp.concatenate([w_qsb, w_vsb, w_vds], axis=1))
    wr_r = jnp.concatenate([w_qds, w_kix4], axis=1)
    wrot_r = jnp.concatenate([_rotate_half_cols(w_qds, D_HEAD), _rotate_half_cols(w_kix4, D_IDX)], axis=1)
    wp_c = bf(w_ksb.T)
    wr_c = jnp.concatenate([w_kds, w_qix], axis=1)
    wrot_c = jnp.concatenate([_rotate_half_cols(w_kds, D_HEAD), _rotate_half_cols(w_qix, D_IDX)], axis=1)

    hn = _prenorm(h, g_mix_pre)
    q_sb, v_sb, v_ds, q_ds, k_ix4, gates = _proj_rows(
        hn, wp_r, bf(wr_r), bf(wrot_r), cos, sin, bf(w_gate), b_gate)
    kt_sb, kt_ds, q_ixt, w_ixt = _proj_cols(
        hn, wp_c, bf(wr_c.T), bf(wrot_c.T), cos.T, sin.T, bf(w_wix.T))

    later_keys = (jnp.arange(BLK)[:, None] > jnp.arange(BLK)[None, :]).astype(BF16)
    o_sb = _sb_attention(q_sb, kt_sb, v_sb, later_keys)

    mask = _select(k_ix4, q_ixt, w_ixt, k_top, t_real)
    o_ds = _dsa_attention(q_ds, kt_ds, v_ds, mask)

    h1 = _merge(o_sb, o_ds, gates, h, bf(w_branch_sb[0]), bf(w_branch_dsa[0]), bf(w_out[0]), g_mix_post)
    h2 = _mlp(h1, g_mlp_pre, bf(w_mlp_in[0]), bf(w_mlp_out[0]), g_mlp_post)
    return h2[None, N_META:N_META + seq]
```

```python
import functools

import jax
import jax.numpy as jnp
from jax import lax
from jax.experimental import pallas as pl
from jax.experimental.pallas import tpu as pltpu

D_MODEL = 1024
D_HEAD = 64
N_HEADS = 8
N_PAIRS = N_HEADS // 2
D_IDX = 32
N_META = 16
K_SEL_MAX = 256
D_FF = 4 * D_MODEL
ROPE_THETA = 500000.0
RMS_EPS = 1e-6

LANES = 128
BLK = 256
SEQ_ALIGN = 512
VMEM_LIMIT = 60 * 1024 * 1024

EXP2_ZERO_BELOW = -151.0
LOG2_E = 1.4426950408889634
NEG_INF = float("-inf")
INT_MIN = -2 ** 31
KEY_OF_NEG_INF = 0x007FFFFF
CELL_SETS = 8
CELL_DEPTH = 4
COUNT_GROUP = 4
SUM_SAFE_LO = 1e-30
SUM_SAFE_HI = 1e30

F32 = jnp.float32
BF16 = jnp.bfloat16


def _dot(a, b):
    return jnp.dot(a, b, preferred_element_type=F32)


def _dot_nt(a, b):
    return lax.dot_general(a, b, (((1,), (1,)), ((), ())), preferred_element_type=F32)


def _rms(x, g):
    return x * lax.rsqrt(jnp.mean(x * x, axis=-1, keepdims=True) + RMS_EPS) * g


def _params(n_axes=1):
    return pltpu.CompilerParams(
        dimension_semantics=("arbitrary",) * n_axes, vmem_limit_bytes=VMEM_LIMIT)


def _resident():
    return pl.BlockSpec(memory_space=pltpu.VMEM)


def _prenorm_kernel(h_ref, g_ref, o_ref):
    o_ref[...] = _rms(h_ref[...], g_ref[...]).astype(BF16)


def _prenorm(h, g):
    tp = h.shape[0]
    tm = SEQ_ALIGN
    return pl.pallas_call(
        _prenorm_kernel,
        grid=(tp // tm,),
        in_specs=[pl.BlockSpec((tm, D_MODEL), lambda i: (i, 0)),
                  pl.BlockSpec((1, D_MODEL), lambda i: (0, 0))],
        out_specs=pl.BlockSpec((tm, D_MODEL), lambda i: (i, 0)),
        out_shape=jax.ShapeDtypeStruct((tp, D_MODEL), BF16),
        compiler_params=_params(),
        name="prenorm",
    )(h, g)


def _proj_rows_kernel(x_ref, wp_ref, wr_ref, wrot_ref, cos_ref, sin_ref, wg_ref, bg_ref,
                      qsb_ref, vsb_ref, vds_ref, qds_ref, kix_ref, gate_ref):
    x = x_ref[...]
    y = _dot(x, wp_ref[...])
    for n, ref in enumerate((qsb_ref, vsb_ref, vds_ref)):
        for g in range(N_PAIRS):
            c = (n * N_PAIRS + g) * LANES
            ref[g] = y[:, c:c + LANES].astype(BF16)
    a = _dot(x, wr_ref[...])
    b = _dot(x, wrot_ref[...])
    cos, sin = cos_ref[...], sin_ref[...]
    cds, sds = cos[:, :LANES], sin[:, :LANES]
    cix, six = cos[:, LANES:], sin[:, LANES:]

    def rope(g, c, s):
        sl = slice(g * LANES, (g + 1) * LANES)
        return (a[:, sl] * c + b[:, sl] * s).astype(BF16)

    for g in range(N_PAIRS):
        qds_ref[g] = rope(g, cds, sds)
    kix_ref[...] = rope(N_PAIRS, cix, six)
    gate_ref[...] = jax.nn.sigmoid(_dot(x, wg_ref[...]) + bg_ref[...]).astype(BF16)


def _proj_rows(hn, wp, wr, wrot, cos, sin, wg, bg):
    tp = hn.shape[0]
    tm = BLK
    full = lambda a: pl.BlockSpec(a.shape, lambda i: (0,) * a.ndim)
    grp = pl.BlockSpec((N_PAIRS, tm, LANES), lambda i: (0, i, 0))
    grp_shape = jax.ShapeDtypeStruct((N_PAIRS, tp, LANES), BF16)
    return pl.pallas_call(
        _proj_rows_kernel,
        grid=(tp // tm,),
        in_specs=[pl.BlockSpec((tm, D_MODEL), lambda i: (i, 0)), full(wp), full(wr), full(wrot),
                  pl.BlockSpec((tm, 2 * LANES), lambda i: (i, 0)),
                  pl.BlockSpec((tm, 2 * LANES), lambda i: (i, 0)),
                  full(wg), full(bg)],
        out_specs=[grp, grp, grp, grp,
                   pl.BlockSpec((tm, LANES), lambda i: (i, 0)),
                   pl.BlockSpec((tm, 2 * D_MODEL), lambda i: (i, 0))],
        out_shape=[grp_shape, grp_shape, grp_shape, grp_shape,
                   jax.ShapeDtypeStruct((tp, LANES), BF16),
                   jax.ShapeDtypeStruct((tp, 2 * D_MODEL), BF16)],
        compiler_params=_params(),
        name="proj_rows",
    )(hn, wp, wr, wrot, cos, sin, wg, bg)


def _proj_cols_kernel(x_ref, wp_ref, wr_ref, wrot_ref, cos_ref, sin_ref, ww_ref,
                      ksb_ref, kds_ref, qix_ref, wix_ref):
    x = x_ref[...]
    y = _dot_nt(wp_ref[...], x)
    for g in range(N_PAIRS):
        ksb_ref[g, 0] = y[g * LANES:(g + 1) * LANES, :].astype(BF16)
    a = _dot_nt(wr_ref[...], x)
    b = _dot_nt(wrot_ref[...], x)
    cos, sin = cos_ref[...], sin_ref[...]
    cds, sds = cos[:LANES, :], sin[:LANES, :]
    cix, six = cos[LANES:, :], sin[LANES:, :]

    def rope(g, c, s):
        sl = slice(g * LANES, (g + 1) * LANES)
        return (a[sl, :] * c + b[sl, :] * s).astype(BF16)

    for g in range(N_PAIRS):
        kds_ref[g, 0] = rope(g, cds, sds)
    for g in range(2):
        qix_ref[g] = rope(N_PAIRS + g, cix, six)
    wix_ref[...] = _dot_nt(ww_ref[...], x)


def _proj_cols(hn, wp, wr, wrot, cos, sin, ww):
    tp = hn.shape[0]
    nkb = tp // BLK
    full = lambda a: pl.BlockSpec(a.shape, lambda i: (0,) * a.ndim)
    kblk = pl.BlockSpec((N_PAIRS, 1, LANES, BLK), lambda i: (0, i, 0, 0))
    return pl.pallas_call(
        _proj_cols_kernel,
        grid=(nkb,),
        in_specs=[pl.BlockSpec((BLK, D_MODEL), lambda i: (i, 0)), full(wp), full(wr), full(wrot),
                  pl.BlockSpec((2 * LANES, BLK), lambda i: (0, i)),
                  pl.BlockSpec((2 * LANES, BLK), lambda i: (0, i)),
                  full(ww)],
        out_specs=[kblk, kblk,
                   pl.BlockSpec((2, LANES, BLK), lambda i: (0, 0, i)),
                   pl.BlockSpec((LANES, BLK), lambda i: (0, i))],
        out_shape=[jax.ShapeDtypeStruct((N_PAIRS, nkb, LANES, BLK), BF16),
                   jax.ShapeDtypeStruct((N_PAIRS, nkb, LANES, BLK), BF16),
                   jax.ShapeDtypeStruct((2, LANES, tp), BF16),
                   jax.ShapeDtypeStruct((LANES, tp), F32)],
        compiler_params=_params(),
        name="proj_cols",
    )(hn, wp, wr, wrot, cos, sin, ww)


def _split_pair_queries(q_ref, qm_scr):
    lane = lax.broadcasted_iota(jnp.int32, (BLK, LANES), 1)
    for j in range(N_PAIRS):
        qp = q_ref[j]
        zero = jnp.zeros_like(qp)
        qm_scr[2 * j] = jnp.where(lane < D_HEAD, qp, zero)
        qm_scr[2 * j + 1] = jnp.where(lane >= D_HEAD, qp, zero)


def _sb_kernel(q_ref, kt_ref, v_ref, u_ref, o_ref, qm_scr, later_scr, acc_scr):
    i = pl.program_id(0)
    _split_pair_queries(q_ref, qm_scr)
    acc_scr[...] = jnp.zeros_like(acc_scr)
    later_scr[...] = jnp.zeros_like(later_scr)
    row = lax.broadcasted_iota(jnp.int32, (BLK, BLK), 0)
    col = lax.broadcasted_iota(jnp.int32, (BLK, BLK), 1)
    visible_diag = col < row
    lane = lax.broadcasted_iota(jnp.int32, (BLK, LANES), 1)
    first_half = lane < D_HEAD
    u = u_ref[...]

    def block(kb, diag):
        start = pl.multiple_of(kb * BLK, BLK)
        worst = None
        for j in range(N_PAIRS):
            kt = kt_ref[j, kb]
            vv = v_ref[j, pl.ds(start, BLK), :]
            pvs = []
            for h in (2 * j, 2 * j + 1):
                z = _dot(qm_scr[h], kt)
                sp = jnp.maximum(z, 0.0) + jnp.log2(1.0 + jnp.exp2(-jnp.abs(z)))
                log_not = -sp
                if diag:
                    log_not = jnp.where(visible_diag, log_not, 0.0)
                inside = _dot(log_not.astype(BF16), u)
                later = later_scr[h]
                a = jnp.exp2(z - sp + inside + jnp.concatenate([later, later], axis=1))
                if diag:
                    a = jnp.where(visible_diag, a, 0.0)
                pvs.append(_dot(a.astype(BF16), vv))
                later = later + jnp.broadcast_to(inside[:, 0:1] + log_not[:, 0:1], (BLK, LANES))
                later_scr[h] = later
                worst = later if worst is None else jnp.maximum(worst, later)
            acc_scr[j] += jnp.where(first_half, pvs[0], pvs[1])
        return jnp.max(worst)

    def cond(carry):
        kb, worst = carry
        return jnp.logical_and(kb >= 0, worst > EXP2_ZERO_BELOW)

    def body(carry):
        kb, _ = carry
        return kb - 1, block(kb, False)

    lax.while_loop(cond, body, (i - 1, block(i, True)))

    for j in range(N_PAIRS):
        o_ref[:, j * LANES:(j + 1) * LANES] = acc_scr[j].astype(BF16)


def _sb_attention(q, kt, v, u):
    tp = q.shape[1]
    return pl.pallas_call(
        _sb_kernel,
        grid=(tp // BLK,),
        in_specs=[pl.BlockSpec((N_PAIRS, BLK, LANES), lambda i: (0, i, 0)),
                  _resident(), _resident(), _resident()],
        out_specs=pl.BlockSpec((BLK, N_PAIRS * LANES), lambda i: (i, 0)),
        out_shape=jax.ShapeDtypeStruct((tp, N_PAIRS * LANES), BF16),
        scratch_shapes=[pltpu.VMEM((N_HEADS, BLK, LANES), BF16),
                        pltpu.VMEM((N_HEADS, BLK, LANES), F32),
                        pltpu.VMEM((N_PAIRS, BLK, LANES), F32)],
        compiler_params=_params(),
        name="sb_attention",
    )(q, kt, v, u)


def _key_to_float(cand):
    neg_top = cand < 0
    bits = jnp.where(neg_top, cand ^ jnp.int32(INT_MIN), ~cand)
    f = lax.bitcast_convert_type(bits, F32)
    return jnp.where(jnp.logical_or(neg_top, cand > KEY_OF_NEG_INF), f, NEG_INF)


def _key16_to_float(p):
    top = p >= 0x8000
    pattern = jnp.where(top, p ^ 0x8000, p ^ 0xFFFF)
    f = lax.bitcast_convert_type(jnp.left_shift(pattern, 16), F32)
    return jnp.where(jnp.logical_or(top, p > (KEY_OF_NEG_INF >> 16)), f, NEG_INF)


def _select_kernel(kix_ref, qixt_ref, wixt_ref, mask_ref, s_scr, shi_scr, colmax_scr, rhs_scr, cell_scr,
                   jst_scr, thr_scr, cgt_scr, ceq_scr, *, k_top, t_real):
    i = pl.program_id(0)
    nblk = i + 1
    nkb = mask_ref.shape[1]
    sub = lax.broadcasted_iota(jnp.int32, (LANES, BLK), 0)
    for h in range(N_HEADS):
        src = qixt_ref[h // 4]
        rhs_scr[h] = jnp.where(sub // D_IDX == h % 4, src, jnp.zeros_like(src))
    w = wixt_ref[0:N_HEADS, :]
    srow = lax.broadcasted_iota(jnp.int32, (BLK, BLK), 0)
    tq = i * BLK + lax.broadcasted_iota(jnp.int32, (BLK, BLK), 1)

    for extra in range(COUNT_GROUP - 1):
        start = pl.multiple_of((nblk + extra) * BLK, BLK)
        s_scr[pl.ds(start, BLK), :] = jnp.full((BLK, BLK), NEG_INF, F32)
        shi_scr[pl.ds(start, BLK), :] = jnp.full((BLK, BLK), NEG_INF, BF16)
    colmax_scr[...] = jnp.full_like(colmax_scr, NEG_INF)
    ngroups = (nblk + COUNT_GROUP - 1) // COUNT_GROUP

    def fill(pair, carry):
        for half in range(2):
            start = pl.multiple_of((2 * pair + half) * BLK, BLK)
            kx = kix_ref[pl.ds(start, BLK), :]
            sc = None
            for h in range(N_HEADS):
                term = w[h:h + 1, :] * jnp.maximum(_dot(kx, rhs_scr[h]), 0.0)
                sc = term if sc is None else sc + term
            sc = jnp.where(start + srow <= tq, sc, NEG_INF)
            s_scr[pl.ds(start, BLK), :] = sc
            shi_scr[pl.ds(start, BLK), :] = sc.astype(BF16)
            colmax_scr[...] = jnp.maximum(colmax_scr[...], sc)
        return carry

    lax.fori_loop(0, (nblk + 1) // 2, fill, 0)

    def reduce_blocks(fn, init, combine):
        def body(grp, acc):
            for g in range(COUNT_GROUP):
                start = pl.multiple_of((grp * COUNT_GROUP + g) * BLK, BLK)
                blk = s_scr[pl.ds(start, BLK), :].reshape(BLK // 8, 8, BLK)
                acc = combine(acc, fn(blk, start))
            return acc
        return lax.fori_loop(0, ngroups, body, init)

    def fold4(m):
        return m.reshape(8, 4, 8, BLK).sum(axis=0)

    def total(acc):
        t = acc.sum(axis=0).sum(axis=0, keepdims=True)
        return jnp.broadcast_to(t, (8, BLK))

    zeros4 = jnp.zeros((4, 8, BLK), F32)
    add = lambda a, b: a + b

    def count(pred):
        return total(reduce_blocks(lambda blk, st: fold4(jnp.where(pred(blk, st), 1.0, 0.0)), zeros4, add))

    k_f = jnp.float32(k_top)

    one_b, zero_b = jnp.ones((), BF16), jnp.zeros((), BF16)

    def count_block_b(blk, thr_b):
        m = jnp.where(blk.reshape(BLK // 16, 16, BLK) >= thr_b[None], one_b, zero_b)
        parts = [m[r] for r in range(BLK // 16)]
        while len(parts) > 1:
            parts = [parts[n] + parts[n + 1] for n in range(0, len(parts), 2)]
        return parts[0].astype(F32)

    def spread(acc16):
        return jnp.broadcast_to(acc16.sum(axis=0, keepdims=True), (8, BLK))

    def count_upper(thr_b):
        def body(grp, acc):
            for g in range(COUNT_GROUP):
                start = pl.multiple_of((grp * COUNT_GROUP + g) * BLK, BLK)
                acc = acc + count_block_b(shi_scr[pl.ds(start, BLK), :], thr_b)
            return acc
        return spread(lax.fori_loop(0, ngroups, body, jnp.zeros((16, BLK), F32)))

    def grid_value_b(cand):
        return jnp.broadcast_to(_key16_to_float(cand)[0:1, :], (16, BLK)).astype(BF16)

    colmax_b = colmax_scr[...].astype(BF16)

    def ends_agree(carry):
        it, p, _ = carry
        cand = p | lax.shift_left(jnp.int32(1), 15 - it)
        cnt = spread(count_block_b(colmax_b, grid_value_b(cand)))
        all_above = cnt >= jnp.float32(BLK)
        settled = jnp.min(jnp.where(all_above | (cnt <= 0.0), 1.0, 0.0)) > 0.0
        return (jnp.where(settled, it + 1, it), jnp.where(settled & all_above, cand, p),
                settled.astype(jnp.int32))

    first_it, p0, _ = lax.while_loop(
        lambda c: jnp.logical_and(c[0] < 16, c[2] > 0), ends_agree,
        (jnp.int32(0), jnp.zeros((8, BLK), jnp.int32), jnp.int32(1)))

    def bisect_upper(it, p):
        cand = p | lax.shift_left(jnp.int32(1), 15 - it)
        return jnp.where(count_upper(grid_value_b(cand)) >= k_f, cand, p)

    p1 = lax.fori_loop(first_it, 16, bisect_upper, p0)
    edges = [_key16_to_float(p1 + d) for d in (-1, 0, 1)]
    counts4 = reduce_blocks(
        lambda blk, st: jnp.stack([fold4(jnp.where(blk >= e[None], 1.0, 0.0)) for e in edges]
                                  + [fold4(jnp.where(blk == 0.0, 1.0, 0.0))]),
        jnp.zeros((4, 4, 8, BLK), F32), add)
    c_lo, c_mid, c_hi, n_zero = (total(counts4[n]) for n in range(4))
    lower = c_mid < k_f
    p = jnp.where(lower, p1 - 1, p1)
    cnt_at_p = jnp.where(lower, c_lo, c_mid)
    cnt_above = jnp.where(lower, c_mid, c_hi)
    bracketed = (c_lo >= k_f) & (c_hi < k_f)
    tlane = i * BLK + lax.broadcasted_iota(jnp.int32, (8, BLK), 1)
    open_row = tlane + 1 < k_top
    bucket_lo = _key16_to_float(p)
    bucket_hi = _key16_to_float(p + 1)
    key_lo = jnp.where(p >= 0x8000, jnp.left_shift(p, 16), jnp.left_shift(p, 16) | 0xFFFF)
    bucket_size = cnt_at_p - cnt_above
    need_in = k_f - cnt_above

    zeros_in = jnp.where((bucket_lo <= 0.0) & (bucket_hi > 0.0), n_zero, 0.0)

    cell_scr[...] = jnp.full_like(cell_scr, NEG_INF)
    per_set = BLK // 8 // CELL_SETS

    def collect(grp, carry):
        xs = []
        for g in range(COUNT_GROUP):
            start = pl.multiple_of((grp * COUNT_GROUP + g) * BLK, BLK)
            blk = s_scr[pl.ds(start, BLK), :].reshape(per_set, CELL_SETS, 8, BLK)
            inside = (blk >= bucket_lo[None, None]) & (blk < bucket_hi[None, None]) & (blk != 0.0)
            xs.append(jnp.where(inside, blk, NEG_INF))
        for s in range(CELL_SETS):
            best = [cell_scr[d, s] for d in range(CELL_DEPTH)]
            for x in xs:
                for a in range(per_set):
                    v = x[a, s]
                    for d in range(CELL_DEPTH - 1):
                        best[d], v = jnp.maximum(best[d], v), jnp.minimum(best[d], v)
                    best[-1] = jnp.maximum(best[-1], v)
            for d in range(CELL_DEPTH):
                cell_scr[d, s] = best[d]
        return carry

    lax.fori_loop(0, ngroups, collect, 0)
    cells = cell_scr[...].reshape(CELL_DEPTH * CELL_SETS, 8, BLK)

    def count_cells(pred):
        t = jnp.where(pred, 1.0, 0.0).sum(axis=0).sum(axis=0, keepdims=True)
        return jnp.broadcast_to(t, (8, BLK))

    def count_bucket(rel):
        return lambda t: count_cells(rel(cells, t[None])) + jnp.where(rel(0.0, t), zeros_in, 0.0)

    ge, gt, eq = (lambda a, b: a >= b), (lambda a, b: a > b), (lambda a, b: a == b)
    kept = count_cells(cells > NEG_INF) + zeros_in
    redo = jnp.logical_not(open_row) & ((kept < bucket_size) | jnp.logical_not(bracketed))

    def bisect_cells(it, low):
        cand = low | lax.shift_left(jnp.int32(1), 15 - it)
        cnt = count_bucket(ge)(_key_to_float(key_lo + cand))
        return jnp.where(cnt >= need_in, cand, low)

    def key_float(key):
        return _key_to_float(jnp.where(open_row, jnp.int32(KEY_OF_NEG_INF), key))

    thr0 = key_float(key_lo + lax.fori_loop(0, 16, bisect_cells, jnp.zeros((8, BLK), jnp.int32)))
    thr_c = jnp.where(cells >= thr0[None], cells, jnp.inf).min(axis=0).min(axis=0, keepdims=True)
    thr_c = jnp.broadcast_to(thr_c, (8, BLK))
    thr_c = jnp.where((zeros_in > 0.0) & (thr0 <= 0.0), jnp.minimum(thr_c, 0.0), thr_c)
    thr_scr[...] = thr_c
    cgt_scr[...] = cnt_above + count_bucket(gt)(thr_c)
    ceq_scr[...] = count_bucket(eq)(thr_c)

    @pl.when(jnp.max(jnp.where(redo, 1.0, 0.0)) > 0.0)
    def _():
        def bisect_all(it, c):
            cand = c | lax.shift_left(jnp.int32(1), 31 - it)
            t = _key_to_float(cand)
            cnt = count(lambda blk, st: blk >= t[None])
            return jnp.where(cnt >= k_f, cand, c)

        t0 = key_float(lax.fori_loop(0, 32, bisect_all, jnp.zeros((8, BLK), jnp.int32)))
        mins = reduce_blocks(
            lambda blk, st: jnp.where(blk >= t0[None], blk, jnp.inf).reshape(8, 4, 8, BLK).min(axis=0),
            jnp.full((4, 8, BLK), jnp.inf, F32), jnp.minimum)
        t = jnp.broadcast_to(mins.min(axis=0).min(axis=0, keepdims=True), (8, BLK))
        both = reduce_blocks(
            lambda blk, st: jnp.stack([fold4(jnp.where(blk > t[None], 1.0, 0.0)),
                                       fold4(jnp.where(blk == t[None], 1.0, 0.0))]),
            jnp.zeros((2, 4, 8, BLK), F32), add)
        thr_scr[...] = t
        cgt_scr[...] = total(both[0])
        ceq_scr[...] = total(both[1])

    thr, c_gt, c_eq = thr_scr[...], cgt_scr[...], ceq_scr[...]
    need = k_f - c_gt
    finite_thr = thr > NEG_INF
    ambiguous = (c_eq > need) & finite_thr & (tlane < t_real)
    jst_scr[...] = jnp.where(finite_thr, jnp.int32(2 ** 30), jnp.int32(-1))

    @pl.when(jnp.max(jnp.where(ambiguous, 1.0, 0.0)) > 0.0)
    def _():
        sidx = lax.broadcasted_iota(jnp.int32, (BLK // 8, 8, BLK), 0) * 8 + \
            lax.broadcasted_iota(jnp.int32, (BLK // 8, 8, BLK), 1)

        def bisect_idx(it, jc):
            cand = jc | lax.shift_left(jnp.int32(1), 14 - it)
            cnt = count(lambda blk, st: (blk == thr[None]) & (st + sidx < cand[None]))
            return jnp.where(cnt < need, cand, jc)

        jc = lax.fori_loop(0, 15, bisect_idx, jnp.zeros((8, BLK), jnp.int32))
        jst_scr[...] = jnp.where(ambiguous, jc, jst_scr[...])

    thr_row = thr[0:1, :]
    jst_row = jst_scr[0:1, :]

    def emit(pair, carry):
        for kb in (2 * pair, 2 * pair + 1):
            start = pl.multiple_of(kb * BLK, BLK)
            blk = s_scr[pl.ds(start, BLK), :]
            sel = (blk > thr_row) | ((blk == thr_row) & (start + srow <= jst_row))
            mask_ref[0, kb] = jnp.where(sel, 0.0, NEG_INF).T.astype(BF16)
        return carry

    npairs = (nblk + 1) // 2
    lax.fori_loop(0, npairs, emit, 0)

    def blank(kb, carry):
        mask_ref[0, kb] = jnp.full((BLK, BLK), NEG_INF, BF16)
        return carry

    lax.fori_loop(2 * npairs, nkb, blank, 0)


def _select(kix, qixt, wixt, k_top, t_real):
    tp = kix.shape[0]
    nkb = tp // BLK
    assert tp < 2 ** 15 and nkb % 2 == 0 and k_top <= BLK
    padded = (nkb + COUNT_GROUP - 1) * BLK
    kern = functools.partial(_select_kernel, k_top=k_top, t_real=t_real)
    return pl.pallas_call(
        kern,
        grid=(nkb,),
        in_specs=[_resident(),
                  pl.BlockSpec((2, LANES, BLK), lambda i: (0, 0, i)),
                  pl.BlockSpec((LANES, BLK), lambda i: (0, i))],
        out_specs=pl.BlockSpec((1, nkb, BLK, BLK), lambda i: (i, 0, 0, 0)),
        out_shape=jax.ShapeDtypeStruct((nkb, nkb, BLK, BLK), BF16),
        scratch_shapes=[pltpu.VMEM((padded, BLK), F32),
                        pltpu.VMEM((padded, BLK), BF16),
                        pltpu.VMEM((BLK, BLK), F32),
                        pltpu.VMEM((N_HEADS, LANES, BLK), BF16),
                        pltpu.VMEM((CELL_DEPTH, CELL_SETS, 8, BLK), F32),
                        pltpu.VMEM((8, BLK), jnp.int32),
                        pltpu.VMEM((8, BLK), F32),
                        pltpu.VMEM((8, BLK), F32),
                        pltpu.VMEM((8, BLK), F32)],
        compiler_params=_params(),
        name="select_mask",
    )(kix, qixt, wixt)


def _dsa_kernel(q_ref, kt_ref, v_ref, mask_ref, o_ref, qm_scr, shift_scr, lsum_scr, acc_scr):
    i = pl.program_id(0)
    _split_pair_queries(q_ref, qm_scr)
    lane = lax.broadcasted_iota(jnp.int32, (BLK, LANES), 1)
    first_half = lane < D_HEAD

    def logits(h, kt, bias):
        return _dot(qm_scr[h], kt) + bias

    def accumulate(shifted):
        lsum_scr[...] = jnp.zeros_like(lsum_scr)
        acc_scr[...] = jnp.zeros_like(acc_scr)

        def blocks(first, count):
            for kb in range(count):
                kb = first + kb
                bias = mask_ref[0, kb].astype(F32)
                start = pl.multiple_of(kb * BLK, BLK)
                for j in range(N_PAIRS):
                    kt = kt_ref[j, kb]
                    vv = v_ref[j, pl.ds(start, BLK), :]
                    pvs = []
                    for h in (2 * j, 2 * j + 1):
                        s = logits(h, kt, bias)
                        if shifted:
                            m = shift_scr[h]
                            s = s - jnp.concatenate([m, m], axis=1)
                        p = jnp.exp(s)
                        lsum_scr[h] += p[:, :LANES] + p[:, LANES:]
                        pvs.append(_dot(p.astype(BF16), vv))
                    acc_scr[j] += jnp.where(first_half, pvs[0], pvs[1])

        npairs = (i + 2) // 2
        nocts = npairs // 4

        def body(octet, carry):
            blocks(8 * octet, 8)
            return carry

        lax.fori_loop(0, nocts, body, 0)

        def tail(pair, carry):
            blocks(2 * pair, 2)
            return carry

        lax.fori_loop(4 * nocts, npairs, tail, 0)

    def row_sums():
        return [jnp.sum(lsum_scr[h], axis=1, keepdims=True) for h in range(N_HEADS)]

    accumulate(False)
    unsafe = jnp.zeros((BLK, 1), F32)
    for l in row_sums():
        unsafe = jnp.maximum(unsafe, jnp.where((l > SUM_SAFE_LO) & (l < SUM_SAFE_HI), 0.0, 1.0))
    acc_big = jnp.where(jnp.abs(acc_scr[...]) < SUM_SAFE_HI, 0.0, 1.0)

    @pl.when((jnp.max(unsafe) > 0.0) | (jnp.max(acc_big) > 0.0))
    def _():
        shift_scr[...] = jnp.full_like(shift_scr, NEG_INF)

        def body(kb, carry):
            bias = mask_ref[0, kb].astype(F32)
            for j in range(N_PAIRS):
                kt = kt_ref[j, kb]
                for h in (2 * j, 2 * j + 1):
                    s = logits(h, kt, bias)
                    shift_scr[h] = jnp.maximum(shift_scr[h], jnp.maximum(s[:, :LANES], s[:, LANES:]))
            return carry

        lax.fori_loop(0, i + 1, body, 0)
        for h in range(N_HEADS):
            m = jnp.max(shift_scr[h], axis=1, keepdims=True)
            m = jnp.where(m == NEG_INF, 0.0, m)
            shift_scr[h] = jnp.broadcast_to(m, (BLK, LANES))
        accumulate(True)

    sums = row_sums()
    for j in range(N_PAIRS):
        denom = jnp.where(first_half, sums[2 * j], sums[2 * j + 1])
        o_ref[:, j * LANES:(j + 1) * LANES] = (acc_scr[j] / denom).astype(BF16)


def _dsa_attention(q, kt, v, mask):
    tp = q.shape[1]
    nkb = tp // BLK
    return pl.pallas_call(
        _dsa_kernel,
        grid=(nkb,),
        in_specs=[pl.BlockSpec((N_PAIRS, BLK, LANES), lambda i: (0, i, 0)),
                  _resident(), _resident(),
                  pl.BlockSpec((1, nkb, BLK, BLK), lambda i: (i, 0, 0, 0))],
        out_specs=pl.BlockSpec((BLK, N_PAIRS * LANES), lambda i: (i, 0)),
        out_shape=jax.ShapeDtypeStruct((tp, N_PAIRS * LANES), BF16),
        scratch_shapes=[pltpu.VMEM((N_HEADS, BLK, LANES), BF16),
                        pltpu.VMEM((N_HEADS, BLK, LANES), F32),
                        pltpu.VMEM((N_HEADS, BLK, LANES), F32),
                        pltpu.VMEM((N_PAIRS, BLK, LANES), F32)],
        compiler_params=_params(),
        name="dsa_attention",
    )(q, kt, v, mask)


def _merge_kernel(osb_ref, ods_ref, gate_ref, h_ref, wsb_ref, wds_ref, wout_ref, g_ref, o_ref):
    gate = gate_ref[...].astype(F32)
    merged = (gate[:, :D_MODEL] * _dot(osb_ref[...], wsb_ref[...])
              + gate[:, D_MODEL:] * _dot(ods_ref[...], wds_ref[...]))
    mix = _dot(merged.astype(BF16), wout_ref[...])
    o_ref[...] = h_ref[...] + _rms(mix, g_ref[...])


def _merge(osb, ods, gates, h, wsb, wds, wout, g):
    tp = h.shape[0]
    tm = BLK
    full = lambda a: pl.BlockSpec(a.shape, lambda i: (0,) * a.ndim)
    rows = lambda n: pl.BlockSpec((tm, n), lambda i: (i, 0))
    return pl.pallas_call(
        _merge_kernel,
        grid=(tp // tm,),
        in_specs=[rows(N_PAIRS * LANES), rows(N_PAIRS * LANES), rows(2 * D_MODEL), rows(D_MODEL),
                  full(wsb), full(wds), full(wout), full(g)],
        out_specs=rows(D_MODEL),
        out_shape=jax.ShapeDtypeStruct((tp, D_MODEL), F32),
        compiler_params=_params(),
        name="merge_out",
    )(osb, ods, gates, h, wsb, wds, wout, g)


def _mlp_kernel(h_ref, gpre_ref, w1_ref, w2_ref, gpost_ref, o_ref):
    h = h_ref[...]
    hn = _rms(h, gpre_ref[...]).astype(BF16)
    ff = None
    for c in range(D_FF // D_MODEL):
        sl = slice(c * D_MODEL, (c + 1) * D_MODEL)
        u = jnp.maximum(_dot(hn, w1_ref[:, sl]), 0.0)
        part = _dot((u * u).astype(BF16), w2_ref[sl, :])
        ff = part if ff is None else ff + part
    o_ref[...] = h + _rms(ff, gpost_ref[...])


def _mlp(h, gpre, w1, w2, gpost):
    tp = h.shape[0]
    tm = BLK
    full = lambda a: pl.BlockSpec(a.shape, lambda i: (0,) * a.ndim)
    rows = pl.BlockSpec((tm, D_MODEL), lambda i: (i, 0))
    return pl.pallas_call(
        _mlp_kernel,
        grid=(tp // tm,),
        in_specs=[rows, full(gpre), full(w1), full(w2), full(gpost)],
        out_specs=rows,
        out_shape=jax.ShapeDtypeStruct((tp, D_MODEL), F32),
        compiler_params=_params(),
        name="mlp",
    )(h, gpre, w1, w2, gpost)


def _rotate_half_cols(w, d_head):
    rot = d_head // 4
    half = rot // 2
    n = w.shape[1] // d_head
    w3 = w.reshape(w.shape[0], n, d_head)
    out = jnp.zeros_like(w3)
    out = out.at[:, :, :half].set(-w3[:, :, half:rot])
    out = out.at[:, :, half:rot].set(w3[:, :, :half])
    return out.reshape(w.shape)


def _rope_tables(tp, d_head):
    rot = d_head // 4
    half = rot // 2
    inv_freq = jnp.power(jnp.float32(ROPE_THETA), -jnp.arange(half, dtype=F32) * (2.0 / rot))
    ang = jnp.arange(tp).astype(F32)[:, None] * inv_freq[None, :]
    cos, sin = jnp.cos(ang), jnp.sin(ang)
    ones = jnp.ones((tp, d_head - rot), F32)
    cos_h = jnp.concatenate([cos, cos, ones], axis=1)
    sin_h = jnp.concatenate([sin, sin, jnp.zeros_like(ones)], axis=1)
    reps = LANES // d_head
    return jnp.tile(cos_h, (1, reps)), jnp.tile(sin_h, (1, reps))


def kernel(x, meta_tokens, w_in, b_gate, w_branch_sb, w_branch_dsa, w_out, g_mix_pre, g_mix_post,
           w_mlp_in, w_mlp_out, g_mlp_pre, g_mlp_post):
    b, seq, d = x.shape
    assert b == 1 and d == D_MODEL and w_in.shape[0] == 1
    k_top = min(K_SEL_MAX, seq // 4)
    t_real = seq + N_META
    tp = -(-t_real // SEQ_ALIGN) * SEQ_ALIGN
    h = jnp.concatenate([meta_tokens.astype(x.dtype), x[0],
                         jnp.zeros((tp - t_real, d), x.dtype)], axis=0)

    w = w_in[0]
    hw = N_HEADS * D_HEAD
    iq = N_HEADS * D_IDX
    o = 0
    w_qsb, w_ksb, w_vsb = w[:, o:o + hw], w[:, o + hw:o + 2 * hw], w[:, o + 2 * hw:o + 3 * hw]
    o += 3 * hw
    w_qds, w_kds, w_vds = w[:, o:o + hw], w[:, o + hw:o + 2 * hw], w[:, o + 2 * hw:o + 3 * hw]
    o += 3 * hw
    w_qix, w_kix, w_wix = w[:, o:o + iq], w[:, o + iq:o + iq + D_IDX], w[:, o + iq + D_IDX:o + iq + D_IDX + N_HEADS]
    o += iq + D_IDX + N_HEADS
    w_gate = w[:, o:o + 2 * D_MODEL]
    w_qsb = w_qsb * (D_HEAD ** -0.5 * LOG2_E)
    w_qds = w_qds * (D_HEAD ** -0.5)
    w_wix = jnp.pad(w_wix * ((D_IDX * N_HEADS) ** -0.5), ((0, 0), (0, LANES - N_HEADS)))
    w_kix4 = jnp.tile(w_kix, (1, LANES // D_IDX))

    cos_ds, sin_ds = _rope_tables(tp, D_HEAD)
    cos_ix, sin_ix = _rope_tables(tp, D_IDX)
    cos = jnp.concatenate([cos_ds, cos_ix], axis=1)
    sin = jnp.concatenate([sin_ds, sin_ix], axis=1)

    bf = lambda a: a.astype(BF16)
    wp_r = bf(jnp.concatenate([w_qsb, w_vsb, w_vds], axis=1))
    wr_r = jnp.concatenate([w_qds, w_kix4], axis=1)
    wrot_r = jnp.concatenate([_rotate_half_cols(w_qds, D_HEAD), _rotate_half_cols(w_kix4, D_IDX)], axis=1)
    wp_c = bf(w_ksb.T)
    wr_c = jnp.concatenate([w_kds, w_qix], axis=1)
    wrot_c = jnp.concatenate([_rotate_half_cols(w_kds, D_HEAD), _rotate_half_cols(w_qix, D_IDX)], axis=1)

    hn = _prenorm(h, g_mix_pre)
    q_sb, v_sb, v_ds, q_ds, k_ix4, gates = _proj_rows(
        hn, wp_r, bf(wr_r), bf(wrot_r), cos, sin, bf(w_gate), b_gate)
    kt_sb, kt_ds, q_ixt, w_ixt = _proj_cols(
        hn, wp_c, bf(wr_c.T), bf(wrot_c.T), cos.T, sin.T, bf(w_wix.T))

    later_keys = (jnp.arange(BLK)[:, None] > jnp.arange(BLK)[None, :]).astype(BF16)
    o_sb = _sb_attention(q_sb, kt_sb, v_sb, later_keys)

    mask = _select(k_ix4, q_ixt, w_ixt, k_top, t_real)
    o_ds = _dsa_attention(q_ds, kt_ds, v_ds, mask)

    h1 = _merge(o_sb, o_ds, gates, h, bf(w_branch_sb[0]), bf(w_branch_dsa[0]), bf(w_out[0]), g_mix_post)
    h2 = _mlp(h1, g_mlp_pre, bf(w_mlp_in[0]), bf(w_mlp_out[0]), g_mlp_post)
    return h2[None, N_META:N_META + seq]
```

```python
import functools

import jax
import jax.numpy as jnp
from jax import lax
from jax.experimental import pallas as pl
from jax.experimental.pallas import tpu as pltpu

D_MODEL = 1024
D_HEAD = 64
N_HEADS = 8
N_PAIRS = N_HEADS // 2
D_IDX = 32
N_META = 16
K_SEL_MAX = 256
D_FF = 4 * D_MODEL
ROPE_THETA = 500000.0
RMS_EPS = 1e-6

LANES = 128
BLK = 256
SEQ_ALIGN = 512
VMEM_LIMIT = 60 * 1024 * 1024

EXP2_ZERO_BELOW = -151.0
LOG2_E = 1.4426950408889634
NEG_INF = float("-inf")
INT_MIN = -2 ** 31
KEY_OF_NEG_INF = 0x007FFFFF
CELL_SETS = 8
CELL_DEPTH = 4
COUNT_GROUP = 4
SUM_SAFE_LO = 1e-30
SUM_SAFE_HI = 1e30

F32 = jnp.float32
BF16 = jnp.bfloat16


def _dot(a, b):
    return jnp.dot(a, b, preferred_element_type=F32)


def _dot_nt(a, b):
    return lax.dot_general(a, b, (((1,), (1,)), ((), ())), preferred_element_type=F32)


def _rms(x, g):
    return x * lax.rsqrt(jnp.mean(x * x, axis=-1, keepdims=True) + RMS_EPS) * g


def _params(n_axes=1):
    return pltpu.CompilerParams(
        dimension_semantics=("arbitrary",) * n_axes, vmem_limit_bytes=VMEM_LIMIT)


def _resident():
    return pl.BlockSpec(memory_space=pltpu.VMEM)


def _prenorm_kernel(h_ref, g_ref, o_ref):
    o_ref[...] = _rms(h_ref[...], g_ref[...]).astype(BF16)


def _prenorm(h, g):
    tp = h.shape[0]
    tm = SEQ_ALIGN
    return pl.pallas_call(
        _prenorm_kernel,
        grid=(tp // tm,),
        in_specs=[pl.BlockSpec((tm, D_MODEL), lambda i: (i, 0)),
                  pl.BlockSpec((1, D_MODEL), lambda i: (0, 0))],
        out_specs=pl.BlockSpec((tm, D_MODEL), lambda i: (i, 0)),
        out_shape=jax.ShapeDtypeStruct((tp, D_MODEL), BF16),
        compiler_params=_params(),
        name="prenorm",
    )(h, g)


def _proj_rows_kernel(x_ref, wp_ref, wr_ref, wrot_ref, cos_ref, sin_ref, wg_ref, bg_ref, g_ref,
                      qsb_ref, vsb_ref, vds_ref, qds_ref, kix_ref, gate_ref):
    x = _rms(x_ref[...], g_ref[...]).astype(BF16)
    y = _dot(x, wp_ref[...])
    for n, ref in enumerate((qsb_ref, vsb_ref, vds_ref)):
        for g in range(N_PAIRS):
            c = (n * N_PAIRS + g) * LANES
            ref[g] = y[:, c:c + LANES].astype(BF16)
    a = _dot(x, wr_ref[...])
    b = _dot(x, wrot_ref[...])
    cos, sin = cos_ref[...], sin_ref[...]
    cds, sds = cos[:, :LANES], sin[:, :LANES]
    cix, six = cos[:, LANES:], sin[:, LANES:]

    def rope(g, c, s):
        sl = slice(g * LANES, (g + 1) * LANES)
        return (a[:, sl] * c + b[:, sl] * s).astype(BF16)

    for g in range(N_PAIRS):
        qds_ref[g] = rope(g, cds, sds)
    kix_ref[...] = rope(N_PAIRS, cix, six)
    gate_ref[...] = jax.nn.sigmoid(_dot(x, wg_ref[...]) + bg_ref[...]).astype(BF16)


def _proj_rows(hn, wp, wr, wrot, cos, sin, wg, bg, g):
    tp = hn.shape[0]
    tm = BLK
    full = lambda a: pl.BlockSpec(a.shape, lambda i: (0,) * a.ndim)
    grp = pl.BlockSpec((N_PAIRS, tm, LANES), lambda i: (0, i, 0))
    grp_shape = jax.ShapeDtypeStruct((N_PAIRS, tp, LANES), BF16)
    return pl.pallas_call(
        _proj_rows_kernel,
        grid=(tp // tm,),
        in_specs=[pl.BlockSpec((tm, D_MODEL), lambda i: (i, 0)), full(wp), full(wr), full(wrot),
                  pl.BlockSpec((tm, 2 * LANES), lambda i: (i, 0)),
                  pl.BlockSpec((tm, 2 * LANES), lambda i: (i, 0)),
                  full(wg), full(bg), full(g)],
        out_specs=[grp, grp, grp, grp,
                   pl.BlockSpec((tm, LANES), lambda i: (i, 0)),
                   pl.BlockSpec((tm, 2 * D_MODEL), lambda i: (i, 0))],
        out_shape=[grp_shape, grp_shape, grp_shape, grp_shape,
                   jax.ShapeDtypeStruct((tp, LANES), BF16),
                   jax.ShapeDtypeStruct((tp, 2 * D_MODEL), BF16)],
        compiler_params=_params(),
        name="proj_rows",
    )(hn, wp, wr, wrot, cos, sin, wg, bg, g)


def _proj_cols_kernel(x_ref, wp_ref, wr_ref, wrot_ref, cos_ref, sin_ref, ww_ref, g_ref,
                      ksb_ref, kds_ref, qix_ref, wix_ref):
    x = _rms(x_ref[...], g_ref[...]).astype(BF16)
    y = _dot_nt(wp_ref[...], x)
    for g in range(N_PAIRS):
        ksb_ref[g, 0] = y[g * LANES:(g + 1) * LANES, :].astype(BF16)
    a = _dot_nt(wr_ref[...], x)
    b = _dot_nt(wrot_ref[...], x)
    cos, sin = cos_ref[...], sin_ref[...]
    cds, sds = cos[:LANES, :], sin[:LANES, :]
    cix, six = cos[LANES:, :], sin[LANES:, :]

    def rope(g, c, s):
        sl = slice(g * LANES, (g + 1) * LANES)
        return (a[sl, :] * c + b[sl, :] * s).astype(BF16)

    for g in range(N_PAIRS):
        kds_ref[g, 0] = rope(g, cds, sds)
    for g in range(2):
        qix_ref[g] = rope(N_PAIRS + g, cix, six)
    wix_ref[...] = _dot_nt(ww_ref[...], x)


def _proj_cols(hn, wp, wr, wrot, cos, sin, ww, g):
    tp = hn.shape[0]
    nkb = tp // BLK
    full = lambda a: pl.BlockSpec(a.shape, lambda i: (0,) * a.ndim)
    kblk = pl.BlockSpec((N_PAIRS, 1, LANES, BLK), lambda i: (0, i, 0, 0))
    return pl.pallas_call(
        _proj_cols_kernel,
        grid=(nkb,),
        in_specs=[pl.BlockSpec((BLK, D_MODEL), lambda i: (i, 0)), full(wp), full(wr), full(wrot),
                  pl.BlockSpec((2 * LANES, BLK), lambda i: (0, i)),
                  pl.BlockSpec((2 * LANES, BLK), lambda i: (0, i)),
                  full(ww), full(g)],
        out_specs=[kblk, kblk,
                   pl.BlockSpec((2, LANES, BLK), lambda i: (0, 0, i)),
                   pl.BlockSpec((LANES, BLK), lambda i: (0, i))],
        out_shape=[jax.ShapeDtypeStruct((N_PAIRS, nkb, LANES, BLK), BF16),
                   jax.ShapeDtypeStruct((N_PAIRS, nkb, LANES, BLK), BF16),
                   jax.ShapeDtypeStruct((2, LANES, tp), BF16),
                   jax.ShapeDtypeStruct((LANES, tp), F32)],
        compiler_params=_params(),
        name="proj_cols",
    )(hn, wp, wr, wrot, cos, sin, ww, g)


def _split_pair_queries(q_ref, qm_scr):
    lane = lax.broadcasted_iota(jnp.int32, (BLK, LANES), 1)
    for j in range(N_PAIRS):
        qp = q_ref[j]
        zero = jnp.zeros_like(qp)
        qm_scr[2 * j] = jnp.where(lane < D_HEAD, qp, zero)
        qm_scr[2 * j + 1] = jnp.where(lane >= D_HEAD, qp, zero)


def _sb_kernel(q_ref, kt_ref, v_ref, u_ref, o_ref, qm_scr, later_scr, acc_scr):
    i = pl.program_id(0)
    _split_pair_queries(q_ref, qm_scr)
    acc_scr[...] = jnp.zeros_like(acc_scr)
    later_scr[...] = jnp.zeros_like(later_scr)
    row = lax.broadcasted_iota(jnp.int32, (BLK, BLK), 0)
    col = lax.broadcasted_iota(jnp.int32, (BLK, BLK), 1)
    visible_diag = col < row
    lane = lax.broadcasted_iota(jnp.int32, (BLK, LANES), 1)
    first_half = lane < D_HEAD
    u = u_ref[...]

    def block(kb, diag):
        start = pl.multiple_of(kb * BLK, BLK)
        worst = None
        for j in range(N_PAIRS):
            kt = kt_ref[j, kb]
            vv = v_ref[j, pl.ds(start, BLK), :]
            pvs = []
            for h in (2 * j, 2 * j + 1):
                z = _dot(qm_scr[h], kt)
                sp = jnp.maximum(z, 0.0) + jnp.log2(1.0 + jnp.exp2(-jnp.abs(z)))
                log_not = -sp
                if diag:
                    log_not = jnp.where(visible_diag, log_not, 0.0)
                inside = _dot(log_not.astype(BF16), u)
                later = later_scr[h]
                a = jnp.exp2(z - sp + inside + jnp.concatenate([later, later], axis=1))
                if diag:
                    a = jnp.where(visible_diag, a, 0.0)
                pvs.append(_dot(a.astype(BF16), vv))
                later = later + jnp.broadcast_to(inside[:, 0:1] + log_not[:, 0:1], (BLK, LANES))
                later_scr[h] = later
                worst = later if worst is None else jnp.maximum(worst, later)
            acc_scr[j] += jnp.where(first_half, pvs[0], pvs[1])
        return jnp.max(worst)

    def cond(carry):
        kb, worst = carry
        return jnp.logical_and(kb >= 0, worst > EXP2_ZERO_BELOW)

    def body(carry):
        kb, _ = carry
        return kb - 1, block(kb, False)

    lax.while_loop(cond, body, (i - 1, block(i, True)))

    for j in range(N_PAIRS):
        o_ref[:, j * LANES:(j + 1) * LANES] = acc_scr[j].astype(BF16)


def _sb_attention(q, kt, v, u):
    tp = q.shape[1]
    return pl.pallas_call(
        _sb_kernel,
        grid=(tp // BLK,),
        in_specs=[pl.BlockSpec((N_PAIRS, BLK, LANES), lambda i: (0, i, 0)),
                  _resident(), _resident(), _resident()],
        out_specs=pl.BlockSpec((BLK, N_PAIRS * LANES), lambda i: (i, 0)),
        out_shape=jax.ShapeDtypeStruct((tp, N_PAIRS * LANES), BF16),
        scratch_shapes=[pltpu.VMEM((N_HEADS, BLK, LANES), BF16),
                        pltpu.VMEM((N_HEADS, BLK, LANES), F32),
                        pltpu.VMEM((N_PAIRS, BLK, LANES), F32)],
        compiler_params=_params(),
        name="sb_attention",
    )(q, kt, v, u)


def _key_to_float(cand):
    neg_top = cand < 0
    bits = jnp.where(neg_top, cand ^ jnp.int32(INT_MIN), ~cand)
    f = lax.bitcast_convert_type(bits, F32)
    return jnp.where(jnp.logical_or(neg_top, cand > KEY_OF_NEG_INF), f, NEG_INF)


def _key16_to_float(p):
    top = p >= 0x8000
    pattern = jnp.where(top, p ^ 0x8000, p ^ 0xFFFF)
    f = lax.bitcast_convert_type(jnp.left_shift(pattern, 16), F32)
    return jnp.where(jnp.logical_or(top, p > (KEY_OF_NEG_INF >> 16)), f, NEG_INF)


def _select_kernel(kix_ref, qixt_ref, wixt_ref, mask_ref, s_scr, shi_scr, colmax_scr, rhs_scr, cell_scr,
                   jst_scr, thr_scr, cgt_scr, ceq_scr, *, k_top, t_real):
    i = pl.program_id(0)
    nblk = i + 1
    nkb = mask_ref.shape[1]
    sub = lax.broadcasted_iota(jnp.int32, (LANES, BLK), 0)
    for h in range(N_HEADS):
        src = qixt_ref[h // 4]
        rhs_scr[h] = jnp.where(sub // D_IDX == h % 4, src, jnp.zeros_like(src))
    w = wixt_ref[0:N_HEADS, :]
    srow = lax.broadcasted_iota(jnp.int32, (BLK, BLK), 0)
    tq = i * BLK + lax.broadcasted_iota(jnp.int32, (BLK, BLK), 1)

    for extra in range(COUNT_GROUP - 1):
        start = pl.multiple_of((nblk + extra) * BLK, BLK)
        s_scr[pl.ds(start, BLK), :] = jnp.full((BLK, BLK), NEG_INF, F32)
        shi_scr[pl.ds(start, BLK), :] = jnp.full((BLK, BLK), NEG_INF, BF16)
    colmax_scr[...] = jnp.full_like(colmax_scr, NEG_INF)
    ngroups = (nblk + COUNT_GROUP - 1) // COUNT_GROUP

    def fill(pair, carry):
        for half in range(2):
            start = pl.multiple_of((2 * pair + half) * BLK, BLK)
            kx = kix_ref[pl.ds(start, BLK), :]
            sc = None
            for h in range(N_HEADS):
                term = w[h:h + 1, :] * jnp.maximum(_dot(kx, rhs_scr[h]), 0.0)
                sc = term if sc is None else sc + term
            sc = jnp.where(start + srow <= tq, sc, NEG_INF)
            s_scr[pl.ds(start, BLK), :] = sc
            shi_scr[pl.ds(start, BLK), :] = sc.astype(BF16)
            colmax_scr[...] = jnp.maximum(colmax_scr[...], sc)
        return carry

    lax.fori_loop(0, (nblk + 1) // 2, fill, 0)

    def reduce_blocks(fn, init, combine):
        def body(grp, acc):
            for g in range(COUNT_GROUP):
                start = pl.multiple_of((grp * COUNT_GROUP + g) * BLK, BLK)
                blk = s_scr[pl.ds(start, BLK), :].reshape(BLK // 8, 8, BLK)
                acc = combine(acc, fn(blk, start))
            return acc
        return lax.fori_loop(0, ngroups, body, init)

    def fold4(m):
        return m.reshape(8, 4, 8, BLK).sum(axis=0)

    def total(acc):
        t = acc.sum(axis=0).sum(axis=0, keepdims=True)
        return jnp.broadcast_to(t, (8, BLK))

    zeros4 = jnp.zeros((4, 8, BLK), F32)
    add = lambda a, b: a + b

    def count(pred):
        return total(reduce_blocks(lambda blk, st: fold4(jnp.where(pred(blk, st), 1.0, 0.0)), zeros4, add))

    k_f = jnp.float32(k_top)

    one_b, zero_b = jnp.ones((), BF16), jnp.zeros((), BF16)

    def count_block_b(blk, thr_b):
        m = jnp.where(blk.reshape(BLK // 16, 16, BLK) >= thr_b[None], one_b, zero_b)
        parts = [m[r] for r in range(BLK // 16)]
        while len(parts) > 1:
            parts = [parts[n] + parts[n + 1] for n in range(0, len(parts), 2)]
        return parts[0].astype(F32)

    def spread(acc16):
        return jnp.broadcast_to(acc16.sum(axis=0, keepdims=True), (8, BLK))

    def count_upper(thr_b):
        def body(grp, acc):
            for g in range(COUNT_GROUP):
                start = pl.multiple_of((grp * COUNT_GROUP + g) * BLK, BLK)
                acc = acc + count_block_b(shi_scr[pl.ds(start, BLK), :], thr_b)
            return acc
        return spread(lax.fori_loop(0, ngroups, body, jnp.zeros((16, BLK), F32)))

    def grid_value_b(cand):
        return jnp.broadcast_to(_key16_to_float(cand)[0:1, :], (16, BLK)).astype(BF16)

    colmax_b = colmax_scr[...].astype(BF16)

    def ends_agree(carry):
        it, p, _ = carry
        cand = p | lax.shift_left(jnp.int32(1), 15 - it)
        cnt = spread(count_block_b(colmax_b, grid_value_b(cand)))
        all_above = cnt >= jnp.float32(BLK)
        settled = jnp.min(jnp.where(all_above | (cnt <= 0.0), 1.0, 0.0)) > 0.0
        return (jnp.where(settled, it + 1, it), jnp.where(settled & all_above, cand, p),
                settled.astype(jnp.int32))

    first_it, p0, _ = lax.while_loop(
        lambda c: jnp.logical_and(c[0] < 16, c[2] > 0), ends_agree,
        (jnp.int32(0), jnp.zeros((8, BLK), jnp.int32), jnp.int32(1)))

    def bisect_upper(it, p):
        cand = p | lax.shift_left(jnp.int32(1), 15 - it)
        return jnp.where(count_upper(grid_value_b(cand)) >= k_f, cand, p)

    p1 = lax.fori_loop(first_it, 16, bisect_upper, p0)
    edges = [_key16_to_float(p1 + d) for d in (-1, 0, 1)]
    counts4 = reduce_blocks(
        lambda blk, st: jnp.stack([fold4(jnp.where(blk >= e[None], 1.0, 0.0)) for e in edges]
                                  + [fold4(jnp.where(blk == 0.0, 1.0, 0.0))]),
        jnp.zeros((4, 4, 8, BLK), F32), add)
    c_lo, c_mid, c_hi, n_zero = (total(counts4[n]) for n in range(4))
    lower = c_mid < k_f
    p = jnp.where(lower, p1 - 1, p1)
    cnt_at_p = jnp.where(lower, c_lo, c_mid)
    cnt_above = jnp.where(lower, c_mid, c_hi)
    bracketed = (c_lo >= k_f) & (c_hi < k_f)
    tlane = i * BLK + lax.broadcasted_iota(jnp.int32, (8, BLK), 1)
    open_row = tlane + 1 < k_top
    bucket_lo = _key16_to_float(p)
    bucket_hi = _key16_to_float(p + 1)
    key_lo = jnp.where(p >= 0x8000, jnp.left_shift(p, 16), jnp.left_shift(p, 16) | 0xFFFF)
    bucket_size = cnt_at_p - cnt_above
    need_in = k_f - cnt_above

    zeros_in = jnp.where((bucket_lo <= 0.0) & (bucket_hi > 0.0), n_zero, 0.0)

    cell_scr[...] = jnp.full_like(cell_scr, NEG_INF)
    per_set = BLK // 8 // CELL_SETS

    def collect(grp, carry):
        xs = []
        for g in range(COUNT_GROUP):
            start = pl.multiple_of((grp * COUNT_GROUP + g) * BLK, BLK)
            blk = s_scr[pl.ds(start, BLK), :].reshape(per_set, CELL_SETS, 8, BLK)
            inside = (blk >= bucket_lo[None, None]) & (blk < bucket_hi[None, None]) & (blk != 0.0)
            xs.append(jnp.where(inside, blk, NEG_INF))
        for s in range(CELL_SETS):
            best = [cell_scr[d, s] for d in range(CELL_DEPTH)]
            for x in xs:
                for a in range(per_set):
                    v = x[a, s]
                    for d in range(CELL_DEPTH - 1):
                        best[d], v = jnp.maximum(best[d], v), jnp.minimum(best[d], v)
                    best[-1] = jnp.maximum(best[-1], v)
            for d in range(CELL_DEPTH):
                cell_scr[d, s] = best[d]
        return carry

    lax.fori_loop(0, ngroups, collect, 0)
    cells = cell_scr[...].reshape(CELL_DEPTH * CELL_SETS, 8, BLK)

    def count_cells(pred):
        t = jnp.where(pred, 1.0, 0.0).sum(axis=0).sum(axis=0, keepdims=True)
        return jnp.broadcast_to(t, (8, BLK))

    def count_bucket(rel):
        return lambda t: count_cells(rel(cells, t[None])) + jnp.where(rel(0.0, t), zeros_in, 0.0)

    ge, gt, eq = (lambda a, b: a >= b), (lambda a, b: a > b), (lambda a, b: a == b)
    kept = count_cells(cells > NEG_INF) + zeros_in
    redo = jnp.logical_not(open_row) & ((kept < bucket_size) | jnp.logical_not(bracketed))

    def bisect_cells(it, low):
        cand = low | lax.shift_left(jnp.int32(1), 15 - it)
        cnt = count_bucket(ge)(_key_to_float(key_lo + cand))
        return jnp.where(cnt >= need_in, cand, low)

    def key_float(key):
        return _key_to_float(jnp.where(open_row, jnp.int32(KEY_OF_NEG_INF), key))

    thr0 = key_float(key_lo + lax.fori_loop(0, 16, bisect_cells, jnp.zeros((8, BLK), jnp.int32)))
    thr_c = jnp.where(cells >= thr0[None], cells, jnp.inf).min(axis=0).min(axis=0, keepdims=True)
    thr_c = jnp.broadcast_to(thr_c, (8, BLK))
    thr_c = jnp.where((zeros_in > 0.0) & (thr0 <= 0.0), jnp.minimum(thr_c, 0.0), thr_c)
    thr_scr[...] = thr_c
    cgt_scr[...] = cnt_above + count_bucket(gt)(thr_c)
    ceq_scr[...] = count_bucket(eq)(thr_c)

    @pl.when(jnp.max(jnp.where(redo, 1.0, 0.0)) > 0.0)
    def _():
        def bisect_all(it, c):
            cand = c | lax.shift_left(jnp.int32(1), 31 - it)
            t = _key_to_float(cand)
            cnt = count(lambda blk, st: blk >= t[None])
            return jnp.where(cnt >= k_f, cand, c)

        t0 = key_float(lax.fori_loop(0, 32, bisect_all, jnp.zeros((8, BLK), jnp.int32)))
        mins = reduce_blocks(
            lambda blk, st: jnp.where(blk >= t0[None], blk, jnp.inf).reshape(8, 4, 8, BLK).min(axis=0),
            jnp.full((4, 8, BLK), jnp.inf, F32), jnp.minimum)
        t = jnp.broadcast_to(mins.min(axis=0).min(axis=0, keepdims=True), (8, BLK))
        both = reduce_blocks(
            lambda blk, st: jnp.stack([fold4(jnp.where(blk > t[None], 1.0, 0.0)),
                                       fold4(jnp.where(blk == t[None], 1.0, 0.0))]),
            jnp.zeros((2, 4, 8, BLK), F32), add)
        thr_scr[...] = t
        cgt_scr[...] = total(both[0])
        ceq_scr[...] = total(both[1])

    thr, c_gt, c_eq = thr_scr[...], cgt_scr[...], ceq_scr[...]
    need = k_f - c_gt
    finite_thr = thr > NEG_INF
    ambiguous = (c_eq > need) & finite_thr & (tlane < t_real)
    jst_scr[...] = jnp.where(finite_thr, jnp.int32(2 ** 30), jnp.int32(-1))

    @pl.when(jnp.max(jnp.where(ambiguous, 1.0, 0.0)) > 0.0)
    def _():
        sidx = lax.broadcasted_iota(jnp.int32, (BLK // 8, 8, BLK), 0) * 8 + \
            lax.broadcasted_iota(jnp.int32, (BLK // 8, 8, BLK), 1)

        def bisect_idx(it, jc):
            cand = jc | lax.shift_left(jnp.int32(1), 14 - it)
            cnt = count(lambda blk, st: (blk == thr[None]) & (st + sidx < cand[None]))
            return jnp.where(cnt < need, cand, jc)

        jc = lax.fori_loop(0, 15, bisect_idx, jnp.zeros((8, BLK), jnp.int32))
        jst_scr[...] = jnp.where(ambiguous, jc, jst_scr[...])

    thr_row = thr[0:1, :]
    jst_row = jst_scr[0:1, :]

    def emit(pair, carry):
        for kb in (2 * pair, 2 * pair + 1):
            start = pl.multiple_of(kb * BLK, BLK)
            blk = s_scr[pl.ds(start, BLK), :]
            sel = (blk > thr_row) | ((blk == thr_row) & (start + srow <= jst_row))
            mask_ref[0, kb] = jnp.where(sel, 0.0, NEG_INF).T.astype(BF16)
        return carry

    npairs = (nblk + 1) // 2
    lax.fori_loop(0, npairs, emit, 0)

    def blank(kb, carry):
        mask_ref[0, kb] = jnp.full((BLK, BLK), NEG_INF, BF16)
        return carry

    lax.fori_loop(2 * npairs, nkb, blank, 0)


def _select(kix, qixt, wixt, k_top, t_real):
    tp = kix.shape[0]
    nkb = tp // BLK
    assert tp < 2 ** 15 and nkb % 2 == 0 and k_top <= BLK
    padded = (nkb + COUNT_GROUP - 1) * BLK
    kern = functools.partial(_select_kernel, k_top=k_top, t_real=t_real)
    return pl.pallas_call(
        kern,
        grid=(nkb,),
        in_specs=[_resident(),
                  pl.BlockSpec((2, LANES, BLK), lambda i: (0, 0, i)),
                  pl.BlockSpec((LANES, BLK), lambda i: (0, i))],
        out_specs=pl.BlockSpec((1, nkb, BLK, BLK), lambda i: (i, 0, 0, 0)),
        out_shape=jax.ShapeDtypeStruct((nkb, nkb, BLK, BLK), BF16),
        scratch_shapes=[pltpu.VMEM((padded, BLK), F32),
                        pltpu.VMEM((padded, BLK), BF16),
                        pltpu.VMEM((BLK, BLK), F32),
                        pltpu.VMEM((N_HEADS, LANES, BLK), BF16),
                        pltpu.VMEM((CELL_DEPTH, CELL_SETS, 8, BLK), F32),
                        pltpu.VMEM((8, BLK), jnp.int32),
                        pltpu.VMEM((8, BLK), F32),
                        pltpu.VMEM((8, BLK), F32),
                        pltpu.VMEM((8, BLK), F32)],
        compiler_params=_params(),
        name="select_mask",
    )(kix, qixt, wixt)


def _dsa_kernel(q_ref, kt_ref, v_ref, mask_ref, o_ref, qm_scr, shift_scr, lsum_scr, acc_scr):
    i = pl.program_id(0)
    _split_pair_queries(q_ref, qm_scr)
    lane = lax.broadcasted_iota(jnp.int32, (BLK, LANES), 1)
    first_half = lane < D_HEAD

    def logits(h, kt, bias):
        return _dot(qm_scr[h], kt) + bias

    def accumulate(shifted):
        lsum_scr[...] = jnp.zeros_like(lsum_scr)
        acc_scr[...] = jnp.zeros_like(acc_scr)

        def blocks(first, count):
            for kb in range(count):
                kb = first + kb
                bias = mask_ref[0, kb].astype(F32)
                start = pl.multiple_of(kb * BLK, BLK)
                for j in range(N_PAIRS):
                    kt = kt_ref[j, kb]
                    vv = v_ref[j, pl.ds(start, BLK), :]
                    pvs = []
                    for h in (2 * j, 2 * j + 1):
                        s = logits(h, kt, bias)
                        if shifted:
                            m = shift_scr[h]
                            s = s - jnp.concatenate([m, m], axis=1)
                        p = jnp.exp(s)
                        lsum_scr[h] += p[:, :LANES] + p[:, LANES:]
                        pvs.append(_dot(p.astype(BF16), vv))
                    acc_scr[j] += jnp.where(first_half, pvs[0], pvs[1])

        npairs = (i + 2) // 2
        nocts = npairs // 4

        def body(octet, carry):
            blocks(8 * octet, 8)
            return carry

        lax.fori_loop(0, nocts, body, 0)

        def tail(pair, carry):
            blocks(2 * pair, 2)
            return carry

        lax.fori_loop(4 * nocts, npairs, tail, 0)

    def row_sums():
        return [jnp.sum(lsum_scr[h], axis=1, keepdims=True) for h in range(N_HEADS)]

    accumulate(False)
    unsafe = jnp.zeros((BLK, 1), F32)
    for l in row_sums():
        unsafe = jnp.maximum(unsafe, jnp.where((l > SUM_SAFE_LO) & (l < SUM_SAFE_HI), 0.0, 1.0))
    acc_big = jnp.where(jnp.abs(acc_scr[...]) < SUM_SAFE_HI, 0.0, 1.0)

    @pl.when((jnp.max(unsafe) > 0.0) | (jnp.max(acc_big) > 0.0))
    def _():
        shift_scr[...] = jnp.full_like(shift_scr, NEG_INF)

        def body(kb, carry):
            bias = mask_ref[0, kb].astype(F32)
            for j in range(N_PAIRS):
                kt = kt_ref[j, kb]
                for h in (2 * j, 2 * j + 1):
                    s = logits(h, kt, bias)
                    shift_scr[h] = jnp.maximum(shift_scr[h], jnp.maximum(s[:, :LANES], s[:, LANES:]))
            return carry

        lax.fori_loop(0, i + 1, body, 0)
        for h in range(N_HEADS):
            m = jnp.max(shift_scr[h], axis=1, keepdims=True)
            m = jnp.where(m == NEG_INF, 0.0, m)
            shift_scr[h] = jnp.broadcast_to(m, (BLK, LANES))
        accumulate(True)

    sums = row_sums()
    for j in range(N_PAIRS):
        denom = jnp.where(first_half, sums[2 * j], sums[2 * j + 1])
        o_ref[:, j * LANES:(j + 1) * LANES] = (acc_scr[j] / denom).astype(BF16)


def _dsa_attention(q, kt, v, mask):
    tp = q.shape[1]
    nkb = tp // BLK
    return pl.pallas_call(
        _dsa_kernel,
        grid=(nkb,),
        in_specs=[pl.BlockSpec((N_PAIRS, BLK, LANES), lambda i: (0, i, 0)),
                  _resident(), _resident(),
                  pl.BlockSpec((1, nkb, BLK, BLK), lambda i: (i, 0, 0, 0))],
        out_specs=pl.BlockSpec((BLK, N_PAIRS * LANES), lambda i: (i, 0)),
        out_shape=jax.ShapeDtypeStruct((tp, N_PAIRS * LANES), BF16),
        scratch_shapes=[pltpu.VMEM((N_HEADS, BLK, LANES), BF16),
                        pltpu.VMEM((N_HEADS, BLK, LANES), F32),
                        pltpu.VMEM((N_HEADS, BLK, LANES), F32),
                        pltpu.VMEM((N_PAIRS, BLK, LANES), F32)],
        compiler_params=_params(),
        name="dsa_attention",
    )(q, kt, v, mask)


def _merge_kernel(osb_ref, ods_ref, gate_ref, h_ref, wsb_ref, wds_ref, wout_ref, g_ref, o_ref):
    gate = gate_ref[...].astype(F32)
    merged = (gate[:, :D_MODEL] * _dot(osb_ref[...], wsb_ref[...])
              + gate[:, D_MODEL:] * _dot(ods_ref[...], wds_ref[...]))
    mix = _dot(merged.astype(BF16), wout_ref[...])
    o_ref[...] = h_ref[...] + _rms(mix, g_ref[...])


def _merge(osb, ods, gates, h, wsb, wds, wout, g):
    tp = h.shape[0]
    tm = BLK
    full = lambda a: pl.BlockSpec(a.shape, lambda i: (0,) * a.ndim)
    rows = lambda n: pl.BlockSpec((tm, n), lambda i: (i, 0))
    return pl.pallas_call(
        _merge_kernel,
        grid=(tp // tm,),
        in_specs=[rows(N_PAIRS * LANES), rows(N_PAIRS * LANES), rows(2 * D_MODEL), rows(D_MODEL),
                  full(wsb), full(wds), full(wout), full(g)],
        out_specs=rows(D_MODEL),
        out_shape=jax.ShapeDtypeStruct((tp, D_MODEL), F32),
        compiler_params=_params(),
        name="merge_out",
    )(osb, ods, gates, h, wsb, wds, wout, g)


def _mlp_kernel(h_ref, gpre_ref, w1_ref, w2_ref, gpost_ref, o_ref):
    h = h_ref[...]
    hn = _rms(h, gpre_ref[...]).astype(BF16)
    ff = None
    for c in range(D_FF // D_MODEL):
        sl = slice(c * D_MODEL, (c + 1) * D_MODEL)
        u = jnp.maximum(_dot(hn, w1_ref[:, sl]), 0.0)
        part = _dot((u * u).astype(BF16), w2_ref[sl, :])
        ff = part if ff is None else ff + part
    o_ref[...] = h + _rms(ff, gpost_ref[...])


def _mlp(h, gpre, w1, w2, gpost):
    tp = h.shape[0]
    tm = BLK
    full = lambda a: pl.BlockSpec(a.shape, lambda i: (0,) * a.ndim)
    rows = pl.BlockSpec((tm, D_MODEL), lambda i: (i, 0))
    return pl.pallas_call(
        _mlp_kernel,
        grid=(tp // tm,),
        in_specs=[rows, full(gpre), full(w1), full(w2), full(gpost)],
        out_specs=rows,
        out_shape=jax.ShapeDtypeStruct((tp, D_MODEL), F32),
        compiler_params=_params(),
        name="mlp",
    )(h, gpre, w1, w2, gpost)


def _rotate_half_cols(w, d_head):
    rot = d_head // 4
    half = rot // 2
    n = w.shape[1] // d_head
    w3 = w.reshape(w.shape[0], n, d_head)
    out = jnp.zeros_like(w3)
    out = out.at[:, :, :half].set(-w3[:, :, half:rot])
    out = out.at[:, :, half:rot].set(w3[:, :, :half])
    return out.reshape(w.shape)


def _rope_tables(tp, d_head):
    rot = d_head // 4
    half = rot // 2
    inv_freq = jnp.power(jnp.float32(ROPE_THETA), -jnp.arange(half, dtype=F32) * (2.0 / rot))
    ang = jnp.arange(tp).astype(F32)[:, None] * inv_freq[None, :]
    cos, sin = jnp.cos(ang), jnp.sin(ang)
    ones = jnp.ones((tp, d_head - rot), F32)
    cos_h = jnp.concatenate([cos, cos, ones], axis=1)
    sin_h = jnp.concatenate([sin, sin, jnp.zeros_like(ones)], axis=1)
    reps = LANES // d_head
    return jnp.tile(cos_h, (1, reps)), jnp.tile(sin_h, (1, reps))


def kernel(x, meta_tokens, w_in, b_gate, w_branch_sb, w_branch_dsa, w_out, g_mix_pre, g_mix_post,
           w_mlp_in, w_mlp_out, g_mlp_pre, g_mlp_post):
    b, seq, d = x.shape
    assert b == 1 and d == D_MODEL and w_in.shape[0] == 1
    k_top = min(K_SEL_MAX, seq // 4)
    t_real = seq + N_META
    tp = -(-t_real // SEQ_ALIGN) * SEQ_ALIGN
    h = jnp.concatenate([meta_tokens.astype(x.dtype), x[0],
                         jnp.zeros((tp - t_real, d), x.dtype)], axis=0)

    w = w_in[0]
    hw = N_HEADS * D_HEAD
    iq = N_HEADS * D_IDX
    o = 0
    w_qsb, w_ksb, w_vsb = w[:, o:o + hw], w[:, o + hw:o + 2 * hw], w[:, o + 2 * hw:o + 3 * hw]
    o += 3 * hw
    w_qds, w_kds, w_vds = w[:, o:o + hw], w[:, o + hw:o + 2 * hw], w[:, o + 2 * hw:o + 3 * hw]
    o += 3 * hw
    w_qix, w_kix, w_wix = w[:, o:o + iq], w[:, o + iq:o + iq + D_IDX], w[:, o + iq + D_IDX:o + iq + D_IDX + N_HEADS]
    o += iq + D_IDX + N_HEADS
    w_gate = w[:, o:o + 2 * D_MODEL]
    w_qsb = w_qsb * (D_HEAD ** -0.5 * LOG2_E)
    w_qds = w_qds * (D_HEAD ** -0.5)
    w_wix = jnp.pad(w_wix * ((D_IDX * N_HEADS) ** -0.5), ((0, 0), (0, LANES - N_HEADS)))
    w_kix4 = jnp.tile(w_kix, (1, LANES // D_IDX))

    cos_ds, sin_ds = _rope_tables(tp, D_HEAD)
    cos_ix, sin_ix = _rope_tables(tp, D_IDX)
    cos = jnp.concatenate([cos_ds, cos_ix], axis=1)
    sin = jnp.concatenate([sin_ds, sin_ix], axis=1)

    bf = lambda a: a.astype(BF16)
    wp_r = bf(jnp.concatenate([w_qsb, w_vsb, w_vds], axis=1))
    wr_r = jnp.concatenate([w_qds, w_kix4], axis=1)
    wrot_r = jnp.concatenate([_rotate_half_cols(w_qds, D_HEAD), _rotate_half_cols(w_kix4, D_IDX)], axis=1)
    wp_c = bf(w_ksb.T)
    wr_c = jnp.concatenate([w_kds, w_qix], axis=1)
    wrot_c = jnp.concatenate([_rotate_half_cols(w_kds, D_HEAD), _rotate_half_cols(w_qix, D_IDX)], axis=1)

    q_sb, v_sb, v_ds, q_ds, k_ix4, gates = _proj_rows(
        h, wp_r, bf(wr_r), bf(wrot_r), cos, sin, bf(w_gate), b_gate, g_mix_pre)
    kt_sb, kt_ds, q_ixt, w_ixt = _proj_cols(
        h, wp_c, bf(wr_c.T), bf(wrot_c.T), cos.T, sin.T, bf(w_wix.T), g_mix_pre)

    later_keys = (jnp.arange(BLK)[:, None] > jnp.arange(BLK)[None, :]).astype(BF16)
    o_sb = _sb_attention(q_sb, kt_sb, v_sb, later_keys)

    mask = _select(k_ix4, q_ixt, w_ixt, k_top, t_real)
    o_ds = _dsa_attention(q_ds, kt_ds, v_ds, mask)

    h1 = _merge(o_sb, o_ds, gates, h, bf(w_branch_sb[0]), bf(w_branch_dsa[0]), bf(w_out[0]), g_mix_post)
    h2 = _mlp(h1, g_mlp_pre, bf(w_mlp_in[0]), bf(w_mlp_out[0]), g_mlp_post)
    return h2[None, N_META:N_META + seq]
```
